```python
import math
import jax, jax.numpy as jnp
from jax import lax
import numpy as np

D_MODEL = 1024
BATCH = 4
SEQ = 8192
DEPTH = 2
DEC_BATCH = 16
DEC_SEQ = 2048
PAST_LEN = 128

F32 = jnp.float32
N_MIXERS = 2
N_HYENA = (DEPTH + 1) // 2
N_NA = DEPTH // 2
DEEPNORM_ALPHA = (2.0 * DEPTH) ** 0.25
DEEPNORM_BETA = (8.0 * DEPTH) ** -0.25
LN_EPS = 1e-5
NEG_INF = -1e30
PLE_DIM = 256

GRID_W = 64

HY_ORDER = 2
HY_DIRS = 2
HY_SHORT = 3
HY_BANDS = 16
HY_EMB = 1 + 2 * HY_BANDS
HY_FILTER_HIDDEN = 64
HY_MIN_RATE = abs(math.log(1e-2)) / 1.5
HY_MAX_RATE = abs(math.log(1e-2)) / 0.3

NA_HEADS = 16
NA_HEAD_DIM = D_MODEL // NA_HEADS
WIN_R = 8
WIN_C = 16
N_COL_BLOCKS = GRID_W // WIN_C
BAND_C = 2 * WIN_C

PEER_HEADS = 8
PEER_KEYS = 128
PEER_EXPERTS = PEER_KEYS * PEER_KEYS
PEER_QDIM = 256
PEER_HALF = PEER_QDIM // 2
PEER_TOPK = 16
PEER_CHUNK = 128

kernel_name = "hyena_natten_peer_deepnorm_encoder"


def layer_norm(x, g, b):
    xf = x.astype(F32)
    mu = jnp.mean(xf, axis=-1, keepdims=True)
    var = jnp.mean(jnp.square(xf - mu), axis=-1, keepdims=True)
    return ((xf - mu) * lax.rsqrt(var + LN_EPS) * g.astype(F32) + b.astype(F32)).astype(x.dtype)


def hyena_filters(L, w1, b1, w2, b2, w3, decay):
    t_norm = jnp.linspace(0.0, 1.0, L, dtype=F32)[:, None]
    t_idx = jnp.arange(L, dtype=F32)[:, None]
    bands = jnp.linspace(1e-4, HY_BANDS - 1, HY_BANDS, dtype=F32)[None, :]
    phase = bands * t_idx * (2.0 * math.pi / L)
    feat = jnp.concatenate([t_norm, jnp.cos(phase), -jnp.sin(phase)], axis=-1)
    h = jnp.sin(feat @ w1.astype(F32) + b1.astype(F32))
    h = jnp.sin(h @ w2.astype(F32) + b2.astype(F32))
    h = (h @ w3.astype(F32)).reshape(L, HY_DIRS, HY_ORDER, D_MODEL)
    window = jnp.exp(-t_norm[:, :, None, None] * jnp.abs(decay.astype(F32)))
    return h * window


def two_sided_kernel(h):
    fwd = h[:, 0]
    bwd = h[1:, 1][::-1]
    return jnp.concatenate([fwd, jnp.zeros_like(fwd[:1]), bwd], axis=0)


def long_conv(z, k_f, skip):
    L = z.shape[1]
    z_f = jnp.fft.rfft(z, n=2 * L, axis=1)
    y = jnp.fft.irfft(z_f * k_f[None], n=2 * L, axis=1)[:, :L]
    return y + skip.astype(F32) * z


def hyena_mixer(x, w_in, b_in, w_conv, b_conv, f_w1, f_b1, f_w2, f_b2, f_w3, decay, skip, w_out, b_out):
    B, L, _ = x.shape
    u = x @ w_in + b_in
    up = jnp.pad(u, ((0, 0), (1, 1), (0, 0)))
    u = up[:, :-2] * w_conv[0] + up[:, 1:-1] * w_conv[1] + up[:, 2:] * w_conv[2] + b_conv
    v, g1, g2 = jnp.split(u.astype(F32), 3, axis=-1)
    h = hyena_filters(L, f_w1, f_b1, f_w2, f_b2, f_w3, decay)
    k_f = jnp.fft.rfft(two_sided_kernel(h), axis=0)
    z = g1 * long_conv(v, k_f[:, 0], skip[0])
    z = g2 * long_conv(z, k_f[:, 1], skip[1])
    return z.astype(x.dtype) @ w_out + b_out


def na_mixer(x, w_qkv, b_qkv, rpb, w_out, b_out):
    B, L, _ = x.shape
    rows = L // GRID_W
    wr = min(WIN_R, rows)
    qkv = (x @ w_qkv + b_qkv).reshape(B, rows, GRID_W, 3, NA_HEADS, NA_HEAD_DIM)
    qkv = qkv.transpose(3, 0, 4, 1, 2, 5)
    q = qkv[0] * (NA_HEAD_DIM ** -0.5)
    k, v = qkv[1], qkv[2]
    q_col = (jnp.arange(N_COL_BLOCKS) * WIN_C)[:, None] + jnp.arange(WIN_C)[None, :]
    band_start = jnp.clip(jnp.arange(N_COL_BLOCKS) * WIN_C - WIN_C // 2, 0, GRID_W - BAND_C)
    band_col = band_start[:, None] + jnp.arange(BAND_C)[None, :]
    win_start = jnp.clip(q_col - WIN_C // 2, 0, GRID_W - WIN_C)[..., None]
    kc = band_col[:, None, :]
    valid = (kc >= win_start) & (kc < win_start + WIN_C)
    dc_idx = jnp.clip(kc - q_col[..., None] + WIN_C - 1, 0, 2 * WIN_C - 2)
    rpb_f = rpb.astype(F32)

    def row_step(r):
        rs = jnp.clip(r - wr // 2, 0, rows - wr)
        kb = lax.dynamic_slice_in_dim(k, rs, wr, axis=2)[:, :, :, band_col]
        vb = lax.dynamic_slice_in_dim(v, rs, wr, axis=2)[:, :, :, band_col]
        qr = lax.dynamic_index_in_dim(q, r, axis=2, keepdims=False)
        qr = qr.reshape(B, NA_HEADS, N_COL_BLOCKS, WIN_C, NA_HEAD_DIM)
        s = jnp.einsum('bhnqd,bhrnkd->bhnqrk', qr, kb).astype(F32)
        dr_idx = rs + jnp.arange(wr) - r + WIN_R - 1
        bias = rpb_f[:, dr_idx[None, None, :, None], dc_idx[:, :, None, :]]
        bias = jnp.where(valid[None, :, :, None, :], bias, NEG_INF)
        s = s + bias[None]
        prob = jax.nn.softmax(s.reshape(B, NA_HEADS, N_COL_BLOCKS, WIN_C, wr * BAND_C), axis=-1)
        prob = prob.reshape(s.shape).astype(vb.dtype)
        return jnp.einsum('bhnqrk,bhrnkd->bhnqd', prob, vb)

    o = lax.map(row_step, jnp.arange(rows))
    o = o.transpose(1, 0, 3, 4, 2, 5).reshape(B, L, D_MODEL)
    return o @ w_out + b_out


def peer_ffn(x, w_q, k1, k2, u_tab, v_tab):
    B, L, D = x.shape
    xt = x.reshape((B * L) // PEER_CHUNK, PEER_CHUNK, D)
    k1f = k1.astype(F32)
    k2f = k2.astype(F32)

    def chunk(xc):
        q = (xc @ w_q).astype(F32).reshape(PEER_CHUNK, PEER_HEADS, PEER_QDIM)
        s1 = jnp.einsum('chd,kd->chk', q[..., :PEER_HALF], k1f)
        s2 = jnp.einsum('chd,kd->chk', q[..., PEER_HALF:], k2f)
        v1, i1 = lax.top_k(s1, PEER_TOPK)
        v2, i2 = lax.top_k(s2, PEER_TOPK)
        cand = (v1[..., :, None] + v2[..., None, :]).reshape(PEER_CHUNK, PEER_HEADS, PEER_TOPK * PEER_TOPK)
        cidx = (i1[..., :, None] * PEER_KEYS + i2[..., None, :]).reshape(PEER_CHUNK, PEER_HEADS, PEER_TOPK * PEER_TOPK)
        top, pos = lax.top_k(cand, PEER_TOPK)
        idx = jnp.take_along_axis(cidx, pos, axis=-1)
        gate = jax.nn.softmax(top, axis=-1)
        act = jax.nn.gelu(jnp.einsum('chkd,cd->chk', u_tab[idx], xc).astype(F32), approximate=False)
        coef = (gate * act).astype(xc.dtype)
        return jnp.einsum('chk,chkd->cd', coef, v_tab[idx])

    return lax.map(chunk, xt).reshape(B, L, D)


def trunk(x, p, w):
    for i in range(DEPTH):
        j = i // N_MIXERS
        if i % N_MIXERS == 0:
            mix = hyena_mixer(x, w['hy_w_in'][j], w['hy_b_in'][j], w['hy_w_conv'][j], w['hy_b_conv'][j],
                              w['hy_f_w1'][j], w['hy_f_b1'][j], w['hy_f_w2'][j], w['hy_f_b2'][j],
                              w['hy_f_w3'][j], w['hy_decay'][j], w['hy_skip'][j],
                              w['hy_w_out'][j], w['hy_b_out'][j])
        else:
            mix = na_mixer(x, w['na_w_qkv'][j], w['na_b_qkv'][j], w['na_rpb'][j],
                           w['na_w_out'][j], w['na_b_out'][j])
        x = layer_norm(DEEPNORM_ALPHA * x + mix, w['ln1_g'][i], w['ln1_b'][i])
        ffn = peer_ffn(x, w['peer_w_q'][i], w['peer_k1'][i], w['peer_k2'][i], w['peer_u'][i], w['peer_v'][i])
        x = layer_norm(DEEPNORM_ALPHA * x + ffn, w['ln2_g'][i], w['ln2_b'][i])
        x = x + jax.nn.sigmoid(x @ w['ple_gate_w'][i]) * (p[i] @ w['ple_w'][i])
    return x


def setup_inputs(seed: int = 0) -> dict:
    key = jax.random.key(seed)
    ks = iter(jax.random.split(key, 48))
    D = D_MODEL

    def nrm(shape, scale):
        return jax.random.normal(next(ks), shape, F32) * scale

    rates = jnp.linspace(HY_MIN_RATE, HY_MAX_RATE, D, dtype=F32)
    hy_decay = jnp.broadcast_to(rates, (N_HYENA, HY_DIRS, HY_ORDER, D)) * (1.0 + nrm((N_HYENA, HY_DIRS, HY_ORDER, D), 0.05))
    return {
        'x_prompt': nrm((BATCH, SEQ, D), 1.0),
        'x_sample': nrm((DEC_BATCH, DEC_SEQ, D), 1.0),
        'p_prompt': nrm((DEPTH, BATCH, SEQ, PLE_DIM), 1.0),
        'p_sample': nrm((DEPTH, DEC_BATCH, DEC_SEQ, PLE_DIM), 1.0),
        'hy_w_in': nrm((N_HYENA, D, 3 * D), D ** -0.5),
        'hy_b_in': nrm((N_HYENA, 3 * D), 0.02),
        'hy_w_conv': nrm((N_HYENA, HY_SHORT, 3 * D), HY_SHORT ** -0.5),
        'hy_b_conv': nrm((N_HYENA, 3 * D), 0.02),
        'hy_f_w1': nrm((N_HYENA, HY_EMB, HY_FILTER_HIDDEN), HY_EMB ** -0.5),
        'hy_f_b1': nrm((N_HYENA, HY_FILTER_HIDDEN), 0.02),
        'hy_f_w2': nrm((N_HYENA, HY_FILTER_HIDDEN, HY_FILTER_HIDDEN), HY_FILTER_HIDDEN ** -0.5),
        'hy_f_b2': nrm((N_HYENA, HY_FILTER_HIDDEN), 0.02),
        'hy_f_w3': nrm((N_HYENA, HY_FILTER_HIDDEN, HY_DIRS * HY_ORDER * D), 0.02 * HY_FILTER_HIDDEN ** -0.5),
        'hy_decay': hy_decay,
        'hy_skip': 1.0 + nrm((N_HYENA, HY_ORDER, D), 0.1),
        'hy_w_out': nrm((N_HYENA, D, D), DEEPNORM_BETA * D ** -0.5),
        'hy_b_out': nrm((N_HYENA, D), 0.02),
        'na_w_qkv': nrm((N_NA, D, 3 * D), D ** -0.5),
        'na_b_qkv': nrm((N_NA, 3 * D), 0.02),
        'na_rpb': nrm((N_NA, NA_HEADS, 2 * WIN_R - 1, 2 * WIN_C - 1), 0.2),
        'na_w_out': nrm((N_NA, D, D), DEEPNORM_BETA * D ** -0.5),
        'na_b_out': nrm((N_NA, D), 0.02),
        'ln1_g': 1.0 + nrm((DEPTH, D), 0.05),
        'ln1_b': nrm((DEPTH, D), 0.02),
        'ln2_g': 1.0 + nrm((DEPTH, D), 0.05),
        'ln2_b': nrm((DEPTH, D), 0.02),
        'peer_w_q': nrm((DEPTH, D, PEER_HEADS * PEER_QDIM), D ** -0.5),
        'peer_k1': nrm((DEPTH, PEER_KEYS, PEER_HALF), PEER_HALF ** -0.5),
        'peer_k2': nrm((DEPTH, PEER_KEYS, PEER_HALF), PEER_HALF ** -0.5),
        'peer_u': nrm((DEPTH, PEER_EXPERTS, D), D ** -0.5),
        'peer_v': nrm((DEPTH, PEER_EXPERTS, D), DEEPNORM_BETA),
        'ple_w': nrm((DEPTH, PLE_DIM, D), DEEPNORM_BETA * PLE_DIM ** -0.5),
        'ple_gate_w': nrm((DEPTH, D, D), D ** -0.5),
    }


def reference(x_prompt, x_sample, p_prompt, p_sample,
              hy_w_in, hy_b_in, hy_w_conv, hy_b_conv, hy_f_w1, hy_f_b1, hy_f_w2, hy_f_b2, hy_f_w3,
              hy_decay, hy_skip, hy_w_out, hy_b_out,
              na_w_qkv, na_b_qkv, na_rpb, na_w_out, na_b_out,
              ln1_g, ln1_b, ln2_g, ln2_b,
              peer_w_q, peer_k1, peer_k2, peer_u, peer_v,
              ple_w, ple_gate_w):
    w = dict(hy_w_in=hy_w_in, hy_b_in=hy_b_in, hy_w_conv=hy_w_conv, hy_b_conv=hy_b_conv,
             hy_f_w1=hy_f_w1, hy_f_b1=hy_f_b1, hy_f_w2=hy_f_w2, hy_f_b2=hy_f_b2, hy_f_w3=hy_f_w3,
             hy_decay=hy_decay, hy_skip=hy_skip, hy_w_out=hy_w_out, hy_b_out=hy_b_out,
             na_w_qkv=na_w_qkv, na_b_qkv=na_b_qkv, na_rpb=na_rpb, na_w_out=na_w_out, na_b_out=na_b_out,
             ln1_g=ln1_g, ln1_b=ln1_b, ln2_g=ln2_g, ln2_b=ln2_b,
             peer_w_q=peer_w_q, peer_k1=peer_k1, peer_k2=peer_k2, peer_u=peer_u, peer_v=peer_v,
             ple_w=ple_w, ple_gate_w=ple_gate_w)
    y_prompt = trunk(x_prompt, p_prompt, w)
    y_sample = trunk(x_sample, p_sample, w)
    return (y_prompt, y_sample)
```

```python
import functools
import math

import jax
import jax.numpy as jnp
from jax import lax
from jax.experimental import pallas as pl
from jax.experimental.pallas import tpu as pltpu

F32 = jnp.float32
BF16 = jnp.bfloat16

D_MODEL = 1024
DEPTH = 2
DEEPNORM_ALPHA = (2.0 * DEPTH) ** 0.25
LN_EPS = 1e-5
NEG_INF = -1e30
PLE_DIM = 256

GRID_W = 64
HY_BANDS = 16
HY_EMB = 1 + 2 * HY_BANDS
HY_HIDDEN = 64
HY_FILTERS = 4

NA_HEADS = 16
NA_HEAD_DIM = D_MODEL // NA_HEADS
WIN_R = 8
WIN_C = 16
NA_HEAD_GROUP = 4

PEER_HEADS = 8
PEER_KEYS = 128
PEER_QDIM = 256
PEER_HALF = PEER_QDIM // 2
PEER_TOPK = 16
PEER_PAIRS = PEER_HEADS * PEER_TOPK

LANES = 128
SUBLANES = 8
FFT_N2 = 128
VMEM_LIMIT = 56 * 1024 * 1024


def _params(*sem):
    return pltpu.CompilerParams(dimension_semantics=sem, vmem_limit_bytes=VMEM_LIMIT)


def _layer_norm(x, g, b):
    mu = jnp.mean(x, axis=-1, keepdims=True)
    xc = x - mu
    var = jnp.mean(xc * xc, axis=-1, keepdims=True)
    return xc * lax.rsqrt(var + LN_EPS) * g + b


def _mm_bias_kernel(x_ref, w_ref, b_ref, s_ref, o_ref):
    acc = jnp.dot(x_ref[...].astype(BF16), w_ref[...], preferred_element_type=F32)
    o_ref[...] = ((acc + b_ref[...]) * s_ref[...]).astype(o_ref.dtype)


def mm_bias(x, w, b, colscale, out_dtype, tm=512):
    t, k = x.shape
    n = w.shape[1]
    tm = min(tm, t)
    return pl.pallas_call(
        _mm_bias_kernel,
        grid=(t // tm,),
        in_specs=[pl.BlockSpec((tm, k), lambda i: (i, 0)),
                  pl.BlockSpec((k, n), lambda i: (0, 0)),
                  pl.BlockSpec((1, n), lambda i: (0, 0)),
                  pl.BlockSpec((1, n), lambda i: (0, 0))],
        out_specs=pl.BlockSpec((tm, n), lambda i: (i, 0)),
        out_shape=jax.ShapeDtypeStruct((t, n), out_dtype),
        compiler_params=_params("parallel"),
        name="mm_bias",
    )(x, w, b, colscale)


def _outproj_ln_kernel(a_ref, w_ref, b_ref, x_ref, g_ref, beta_ref, o_ref):
    mix = jnp.dot(a_ref[...].astype(BF16), w_ref[...], preferred_element_type=F32) + b_ref[...]
    o_ref[...] = _layer_norm(DEEPNORM_ALPHA * x_ref[...] + mix, g_ref[...], beta_ref[...])


def outproj_ln(a, w, b, x, g, beta, tm=512):
    t, d = x.shape
    tm = min(tm, t)
    row = lambda i: (i, 0)
    fix = lambda i: (0, 0)
    return pl.pallas_call(
        _outproj_ln_kernel,
        grid=(t // tm,),
        in_specs=[pl.BlockSpec((tm, d), row), pl.BlockSpec((d, d), fix), pl.BlockSpec((1, d), fix),
                  pl.BlockSpec((tm, d), row), pl.BlockSpec((1, d), fix), pl.BlockSpec((1, d), fix)],
        out_specs=pl.BlockSpec((tm, d), row),
        out_shape=jax.ShapeDtypeStruct((t, d), F32),
        compiler_params=_params("parallel"),
        name="outproj_ln",
    )(a, w, b, x, g, beta)


def _ln_ple_kernel(x_ref, f_ref, p_ref, wg_ref, wp_ref, g_ref, beta_ref, o_ref):
    x2 = _layer_norm(DEEPNORM_ALPHA * x_ref[...] + f_ref[...], g_ref[...], beta_ref[...])
    gate = jax.nn.sigmoid(jnp.dot(x2.astype(BF16), wg_ref[...], preferred_element_type=F32))
    emb = jnp.dot(p_ref[...].astype(BF16), wp_ref[...], preferred_element_type=F32)
    o_ref[...] = x2 + gate * emb


def ln_ple(x, ffn, p, wg, wp, g, beta, tm=512):
    t, d = x.shape
    tm = min(tm, t)
    row = lambda i: (i, 0)
    fix = lambda i: (0, 0)
    return pl.pallas_call(
        _ln_ple_kernel,
        grid=(t // tm,),
        in_specs=[pl.BlockSpec((tm, d), row), pl.BlockSpec((tm, d), row), pl.BlockSpec((tm, PLE_DIM), row),
                  pl.BlockSpec((d, d), fix), pl.BlockSpec((PLE_DIM, d), fix),
                  pl.BlockSpec((1, d), fix), pl.BlockSpec((1, d), fix)],
        out_specs=pl.BlockSpec((tm, d), row),
        out_shape=jax.ShapeDtypeStruct((t, d), F32),
        compiler_params=_params("parallel"),
        name="ln_ple",
    )(x, ffn, p, wg, wp, g, beta)


def _hy_inproj_kernel(x_ref, xp_ref, xn_ref, w_ref, b_ref, wc_ref, bc_ref, o_ref, *, tm, seq):
    i = pl.program_id(1)
    w = w_ref[...]
    b = b_ref[...]
    u = jnp.dot(x_ref[...].astype(BF16), w, preferred_element_type=F32) + b
    up = jnp.dot(xp_ref[...].astype(BF16), w, preferred_element_type=F32) + b
    un = jnp.dot(xn_ref[...].astype(BF16), w, preferred_element_type=F32) + b
    first = (i * tm) % seq == 0
    last = ((i + 1) * tm) % seq == 0
    prev_row = jnp.where(first, 0.0, up[SUBLANES - 1:SUBLANES, :])
    next_row = jnp.where(last, 0.0, un[0:1, :])
    rows = lax.broadcasted_iota(jnp.int32, u.shape, 0)
    um1 = jnp.where(rows == 0, prev_row, pltpu.roll(u, 1, 0))
    up1 = jnp.where(rows == tm - 1, next_row, pltpu.roll(u, tm - 1, 0))
    wc = wc_ref[...]
    o_ref[...] = um1 * wc[0:1, :] + u * wc[1:2, :] + up1 * wc[2:3, :] + bc_ref[...]


def hy_inproj(x, w, b, wc, bc, seq, tm=512):
    t, d = x.shape
    tm = min(tm, seq)
    nblk8 = t // SUBLANES
    per = tm // SUBLANES
    kern = functools.partial(_hy_inproj_kernel, tm=tm, seq=seq)
    return pl.pallas_call(
        kern,
        grid=(3, t // tm),
        in_specs=[pl.BlockSpec((tm, d), lambda j, i: (i, 0)),
                  pl.BlockSpec((SUBLANES, d), lambda j, i: (jnp.maximum(i * per - 1, 0), 0)),
                  pl.BlockSpec((SUBLANES, d), lambda j, i: (jnp.minimum((i + 1) * per, nblk8 - 1), 0)),
                  pl.BlockSpec((d, d), lambda j, i: (0, j)),
                  pl.BlockSpec((1, d), lambda j, i: (0, j)),
                  pl.BlockSpec((3, d), lambda j, i: (0, j)),
                  pl.BlockSpec((1, d), lambda j, i: (0, j))],
        out_specs=pl.BlockSpec((None, tm, d), lambda j, i: (j, i, 0)),
        out_shape=jax.ShapeDtypeStruct((3, t, d), F32),
        compiler_params=_params("parallel", "parallel"),
        name="hy_inproj",
    )(x, x, x, w, b, wc, bc)


def _hy_filter_kernel(feat_ref, w1_ref, b1_ref, w2_ref, b2_ref, w3_ref, dec_ref, o_ref, *, tl):
    hi = lax.Precision.HIGHEST
    feat = feat_ref[...]
    h = jnp.sin(jnp.dot(feat, w1_ref[...], preferred_element_type=F32, precision=hi) + b1_ref[...])
    h = jnp.sin(jnp.dot(h, w2_ref[...], preferred_element_type=F32, precision=hi) + b2_ref[...])
    h = jnp.dot(h, w3_ref[...], preferred_element_type=F32, precision=hi)
    h = h * jnp.exp(-feat[:, 0:1] * jnp.abs(dec_ref[...]))
    pos = pl.program_id(0) * tl + lax.broadcasted_iota(jnp.int32, (tl, D_MODEL), 0)
    for f in range(HY_FILTERS):
        piece = h[:, f * D_MODEL:(f + 1) * D_MODEL]
        if f >= HY_FILTERS // 2:
            piece = jnp.where(pos == 0, 0.0, piece)
        o_ref[f] = piece


def hy_filter(seq, w1, b1, w2, b2, w3, decay, tl=256):
    tl = min(tl, seq)
    t_norm = jnp.linspace(0.0, 1.0, seq, dtype=F32)[:, None]
    t_idx = jnp.arange(seq, dtype=F32)[:, None]
    bands = jnp.linspace(1e-4, HY_BANDS - 1, HY_BANDS, dtype=F32)[None, :]
    phase = bands * t_idx * (2.0 * math.pi / seq)
    feat = jnp.concatenate([t_norm, jnp.cos(phase), -jnp.sin(phase)], axis=-1)
    feat = jnp.pad(feat, ((0, 0), (0, LANES - HY_EMB)))
    padh = LANES - HY_HIDDEN
    w1p = jnp.pad(w1, ((0, LANES - HY_EMB), (0, padh)))
    b1p = jnp.pad(b1, (0, padh))[None, :]
    w2p = jnp.pad(w2, ((0, padh), (0, padh)))
    b2p = jnp.pad(b2, (0, padh))[None, :]
    w3p = jnp.pad(w3, ((0, padh), (0, 0)))
    nf = HY_FILTERS * D_MODEL
    fix = lambda i: (0, 0)
    return pl.pallas_call(
        functools.partial(_hy_filter_kernel, tl=tl),
        grid=(seq // tl,),
        in_specs=[pl.BlockSpec((tl, LANES), lambda i: (i, 0)),
                  pl.BlockSpec((LANES, LANES), fix), pl.BlockSpec((1, LANES), fix),
                  pl.BlockSpec((LANES, LANES), fix), pl.BlockSpec((1, LANES), fix),
                  pl.BlockSpec((LANES, nf), fix), pl.BlockSpec((1, nf), fix)],
        out_specs=pl.BlockSpec((HY_FILTERS, tl, D_MODEL), lambda i: (0, i, 0)),
        out_shape=jax.ShapeDtypeStruct((HY_FILTERS, seq, D_MODEL), F32),
        compiler_params=_params("parallel"),
        name="hy_filter",
    )(feat, w1p, b1p, w2p, b2p, w3p, decay.reshape(1, nf))


def _dft_tables(seq):
    n = 2 * seq
    n2 = FFT_N2
    n1 = n // n2
    n1h = n1 // 2
    two_pi = 2.0 * math.pi

    def ang(m, period):
        return (m % period).astype(F32) * (two_pi / period)

    i_n2 = jnp.arange(n2, dtype=jnp.int32)[:, None, None]
    i_k1 = jnp.arange(n1, dtype=jnp.int32)[None, :, None]
    i_n1 = jnp.arange(n1h, dtype=jnp.int32)[None, None, :]
    th = ang(i_k1 * (n2 * i_n1 + i_n2), n)
    fa = jnp.concatenate([jnp.cos(th), -jnp.sin(th)], axis=1)
    a = jnp.arange(n2, dtype=jnp.int32)
    ph = ang(a[:, None] * a[None, :], n2)
    c, s = jnp.cos(ph), jnp.sin(ph)
    mf = jnp.concatenate([jnp.concatenate([c, s], axis=1), jnp.concatenate([-s, c], axis=1)], axis=0)
    j_k1 = jnp.arange(n1, dtype=jnp.int32)[:, None, None]
    j_n2 = jnp.arange(n2, dtype=jnp.int32)[None, :, None]
    j_k2 = jnp.arange(n2, dtype=jnp.int32)[None, None, :]
    phi = ang(j_n2 * (j_k1 + n1 * j_k2), n)
    c, s = jnp.cos(phi), jnp.sin(phi)
    gi = jnp.concatenate([jnp.concatenate([c, -s], axis=2), jnp.concatenate([s, c], axis=2)], axis=1)
    r_n1 = jnp.arange(n1h, dtype=jnp.int32)[:, None]
    r_k1 = jnp.arange(n1, dtype=jnp.int32)[None, :]
    psi = ang(r_n1 * r_k1, n1)
    fb = jnp.concatenate([jnp.cos(psi), -jnp.sin(psi)], axis=1) * (1.0 / n)
    return dict(n1=n1, n2=n2, n1h=n1h, fa=fa, mf=mf, gi=gi, fb=fb)


def _dot(a, b, exact):
    if exact:
        return jnp.dot(a, b, preferred_element_type=F32, precision=lax.Precision.HIGHEST)
    return jnp.dot(a, b.astype(BF16), preferred_element_type=F32)


def _fft_a_kernel(x_ref, fa_ref, o_ref, *, n1, exact):
    for j in range(SUBLANES):
        r = _dot(fa_ref[j], x_ref[:, j, :], exact)
        o_ref[0, :, j, :] = r[:n1]
        o_ref[1, :, j, :] = r[n1:]


def fft_stage_a(x, tab, exact=False):
    bz, seq, d = x.shape
    n1, n2, n1h = tab["n1"], tab["n2"], tab["n1h"]
    g8 = n2 // SUBLANES
    fa = tab["fa"] if exact else tab["fa"].astype(BF16)
    x5 = x.reshape(bz, n1h, g8, SUBLANES, d)
    out = pl.pallas_call(
        functools.partial(_fft_a_kernel, n1=n1, exact=exact),
        grid=(bz, g8),
        in_specs=[pl.BlockSpec((None, n1h, None, SUBLANES, d), lambda b, g: (b, 0, g, 0, 0)),
                  pl.BlockSpec((SUBLANES, 2 * n1, n1h), lambda b, g: (g, 0, 0))],
        out_specs=pl.BlockSpec((None, 2, n1, None, SUBLANES, d), lambda b, g: (b, 0, 0, g, 0, 0)),
        out_shape=jax.ShapeDtypeStruct((bz, 2, n1, g8, SUBLANES, d), F32),
        compiler_params=_params("parallel", "parallel"),
        name="fft_stage_a",
    )(x5, fa)
    return out.reshape(bz, 2, n1, n2, d)


def _filter_spec_kernel(cf_ref, cb_ref, mf_ref, o_ref, *, n2):
    mf = mf_ref[...]
    yf = _dot(mf, jnp.concatenate([cf_ref[0], cf_ref[1]], axis=0), True)
    yb = _dot(mf, jnp.concatenate([cb_ref[0], cb_ref[1]], axis=0), True)
    o_ref[0] = yf[:n2] + yb[:n2]
    o_ref[1] = yf[n2:] - yb[n2:]


def filter_spectrum(cf, tab):
    n1, n2 = tab["n1"], tab["n2"]
    d = cf.shape[-1]
    norder = HY_FILTERS // 2
    blk = (None, 2, None, n2, d)
    return pl.pallas_call(
        functools.partial(_filter_spec_kernel, n2=n2),
        grid=(norder, n1),
        in_specs=[pl.BlockSpec(blk, lambda o, k: (o, 0, k, 0, 0)),
                  pl.BlockSpec(blk, lambda o, k: (norder + o, 0, k, 0, 0)),
                  pl.BlockSpec((2 * n2, 2 * n2), lambda o, k: (0, 0))],
        out_specs=pl.BlockSpec(blk, lambda o, k: (o, 0, k, 0, 0)),
        out_shape=jax.ShapeDtypeStruct((norder, 2, n1, n2, d), F32),
        compiler_params=_params("parallel", "parallel"),
        name="filter_spectrum",
    )(cf, cf, tab["mf"])


def _fft_mid_kernel(c_ref, kf_ref, mf_ref, gi_ref, o_ref, *, n2):
    c = jnp.concatenate([c_ref[0], c_ref[1]], axis=0)
    y = _dot(mf_ref[...], c, False)
    yr, yi = y[:n2], y[n2:]
    kr, ki = kf_ref[0], kf_ref[1]
    z = jnp.concatenate([yr * kr - yi * ki, yr * ki + yi * kr], axis=0)
    dd = _dot(gi_ref[...], z, False)
    o_ref[0] = dd[:n2]
    o_ref[1] = dd[n2:]


def fft_mid(c, kf, tab):
    bz, _, n1, n2, d = c.shape
    blk = (None, 2, None, n2, d)
    return pl.pallas_call(
        functools.partial(_fft_mid_kernel, n2=n2),
        grid=(n1, bz),
        in_specs=[pl.BlockSpec(blk, lambda k, b: (b, 0, k, 0, 0)),
                  pl.BlockSpec((2, None, n2, d), lambda k, b: (0, k, 0, 0)),
                  pl.BlockSpec((2 * n2, 2 * n2), lambda k, b: (0, 0)),
                  pl.BlockSpec((None, 2 * n2, 2 * n2), lambda k, b: (k, 0, 0))],
        out_specs=pl.BlockSpec(blk, lambda k, b: (b, 0, k, 0, 0)),
        out_shape=jax.ShapeDtypeStruct((bz, 2, n1, n2, d), F32),
        compiler_params=_params("parallel", "parallel"),
        name="fft_mid",
    )(c, kf, tab["mf"].astype(BF16), tab["gi"].astype(BF16))


def _fft_b_kernel(d_ref, fb_ref, z_ref, g_ref, skip_ref, o_ref):
    fb = fb_ref[...]
    skip = skip_ref[...]
    for j in range(SUBLANES):
        dd = jnp.concatenate([d_ref[0, :, j, :], d_ref[1, :, j, :]], axis=0)
        y = _dot(fb, dd, False)
        o_ref[:, j, :] = (y + skip * z_ref[:, j, :]) * g_ref[:, j, :]


def fft_stage_b(dp, tab, z, gate, skip):
    bz, _, n1, n2, d = dp.shape
    n1h = tab["n1h"]
    g8 = n2 // SUBLANES
    seq = z.shape[1]
    dp6 = dp.reshape(bz, 2, n1, g8, SUBLANES, d)
    view = lambda a: a.reshape(bz, n1h, g8, SUBLANES, d)
    tok = pl.BlockSpec((None, n1h, None, SUBLANES, d), lambda b, g: (b, 0, g, 0, 0))
    out = pl.pallas_call(
        _fft_b_kernel,
        grid=(bz, g8),
        in_specs=[pl.BlockSpec((None, 2, n1, None, SUBLANES, d), lambda b, g: (b, 0, 0, g, 0, 0)),
                  pl.BlockSpec((n1h, 2 * n1), lambda b, g: (0, 0)),
                  tok, tok,
                  pl.BlockSpec((1, d), lambda b, g: (0, 0))],
        out_specs=tok,
        out_shape=jax.ShapeDtypeStruct((bz, n1h, g8, SUBLANES, d), F32),
        compiler_params=_params("parallel", "parallel"),
        name="fft_stage_b",
    )(dp6, tab["fb"].astype(BF16), view(z), view(gate), skip)
    return out.reshape(bz, seq, d)


def hyena_mixer(xf, bsz, seq, w, j):
    d = D_MODEL
    uvg = hy_inproj(xf, w["hy_w_in"][j].astype(BF16), w["hy_b_in"][j][None, :],
                    w["hy_w_conv"][j], w["hy_b_conv"][j][None, :], seq)
    uvg = uvg.reshape(3, bsz, seq, d)
    tab = _dft_tables(seq)
    hk = hy_filter(seq, w["hy_f_w1"][j], w["hy_f_b1"][j], w["hy_f_w2"][j], w["hy_f_b2"][j],
                   w["hy_f_w3"][j], w["hy_decay"][j])
    kf = filter_spectrum(fft_stage_a(hk, tab, exact=True), tab)
    z = uvg[0]
    for order in range(2):
        c = fft_stage_a(z, tab)
        dp = fft_mid(c, kf[order], tab)
        z = fft_stage_b(dp, tab, z, uvg[1 + order], w["hy_skip"][j][order][None, :])
    return z.reshape(bsz * seq, d)


def _na_bias_kernel(rpb_ref, o_ref):
    h = pl.program_id(0)
    q = lax.broadcasted_iota(jnp.int32, (GRID_W, GRID_W), 0)
    kc = lax.broadcasted_iota(jnp.int32, (GRID_W, GRID_W), 1)
    win_start = jnp.clip(q - WIN_C // 2, 0, GRID_W - WIN_C)
    valid = (kc >= win_start) & (kc < win_start + WIN_C)
    dc = jnp.clip(kc - q + WIN_C - 1, 0, 2 * WIN_C - 2)
    nrow = 2 * WIN_R - 1
    for dr in range(nrow):
        t = jnp.zeros((GRID_W, GRID_W), F32)
        for c in range(2 * WIN_C - 1):
            t = jnp.where(dc == c, rpb_ref[h * nrow + dr, c], t)
        o_ref[dr] = jnp.where(valid, t, NEG_INF)


def na_bias_table(rpb):
    nrow = 2 * WIN_R - 1
    ncol = 2 * WIN_C - 1
    return pl.pallas_call(
        _na_bias_kernel,
        grid=(NA_HEADS,),
        in_specs=[pl.BlockSpec(memory_space=pltpu.SMEM)],
        out_specs=pl.BlockSpec((None, nrow, GRID_W, GRID_W), lambda h: (h, 0, 0, 0)),
        out_shape=jax.ShapeDtypeStruct((NA_HEADS, nrow, GRID_W, GRID_W), F32),
        compiler_params=_params("parallel"),
        name="na_bias",
    )(rpb.reshape(NA_HEADS * nrow, ncol))


def _na_attn_kernel(q_ref, k_ref, v_ref, bias_ref, o_ref, *, rows):
    r = pl.program_id(2)
    rs = jnp.clip(r - WIN_R // 2, 0, rows - WIN_R)
    start = pl.multiple_of(rs * GRID_W, GRID_W)
    nk = WIN_R * GRID_W
    kb = k_ref[pl.ds(start, nk), :]
    vb = v_ref[pl.ds(start, nk), :]
    q = q_ref[...]
    outs = []
    for hh in range(NA_HEAD_GROUP):
        sl = slice(hh * NA_HEAD_DIM, (hh + 1) * NA_HEAD_DIM)
        s = lax.dot_general(q[:, sl], kb[:, sl], (((1,), (1,)), ((), ())), preferred_element_type=F32)
        s = s + bias_ref[hh]
        m = jnp.max(s, axis=-1, keepdims=True)
        e = jnp.exp(s - m)
        p = e / jnp.sum(e, axis=-1, keepdims=True)
        outs.append(jnp.dot(p.astype(BF16), vb[:, sl], preferred_element_type=F32))
    o_ref[...] = jnp.concatenate(outs, axis=1)


def na_attention(qkv, bias, bsz, seq):
    rows = seq // GRID_W
    t = bsz * seq
    gl = NA_HEAD_GROUP * NA_HEAD_DIM
    ngrp = NA_HEADS // NA_HEAD_GROUP

    def offset(r):
        return r - jnp.clip(r - WIN_R // 2, 0, rows - WIN_R)

    return pl.pallas_call(
        functools.partial(_na_attn_kernel, rows=rows),
        grid=(bsz, ngrp, rows),
        in_specs=[pl.BlockSpec((GRID_W, gl), lambda b, g, r: (b * rows + r, g)),
                  pl.BlockSpec((seq, gl), lambda b, g, r: (b, ngrp + g)),
                  pl.BlockSpec((seq, gl), lambda b, g, r: (b, 2 * ngrp + g)),
                  pl.BlockSpec((NA_HEAD_GROUP, None, GRID_W, WIN_R * GRID_W), lambda b, g, r: (g, offset(r), 0, 0))],
        out_specs=pl.BlockSpec((GRID_W, gl), lambda b, g, r: (b * rows + r, g)),
        out_shape=jax.ShapeDtypeStruct((t, D_MODEL), F32),
        compiler_params=_params("parallel", "parallel", "arbitrary"),
        name="na_attention",
    )(qkv, qkv, qkv, bias)


def na_mixer(xf, bsz, seq, w, j):
    d = D_MODEL
    colscale = jnp.concatenate([jnp.full((1, d), NA_HEAD_DIM ** -0.5, F32), jnp.ones((1, 2 * d), F32)], axis=1)
    qkv = mm_bias(xf, w["na_w_qkv"][j].astype(BF16), w["na_b_qkv"][j][None, :], colscale, BF16)
    tb = na_bias_table(w["na_rpb"][j])
    dr = jnp.arange(WIN_R)[None, :] - jnp.arange(WIN_R)[:, None] + WIN_R - 1
    bias = tb[:, dr]
    bias = bias.transpose(0, 1, 3, 2, 4).reshape(NA_HEADS, WIN_R, GRID_W, WIN_R * GRID_W)
    return na_attention(qkv, bias, bsz, seq)


def _top_rounds(s, n_round, val_ref, idx_ref, payload=None):
    nrow, ncol = s.shape
    rowid = lax.broadcasted_iota(jnp.int32, (nrow, ncol), 0)

    def body(r, s):
        m = jnp.max(s, axis=0, keepdims=True)
        am = jnp.min(jnp.where(s == m, rowid, nrow), axis=0, keepdims=True)
        hit = rowid == am
        val_ref[pl.ds(r, 1), :] = m
        if payload is None:
            idx_ref[pl.ds(r, 1), :] = am
        else:
            idx_ref[pl.ds(r, 1), :] = jnp.max(jnp.where(hit, payload, -1), axis=0, keepdims=True)
        return jnp.where(hit, -jnp.inf, s)

    lax.fori_loop(0, n_round, body, s)


def _peer_route_kernel(x_ref, wq_ref, k1_ref, k2_ref, idx_ref, gate_ref, v1, i1, v2, i2, vt, it):
    q = jnp.dot(x_ref[...].astype(BF16), wq_ref[...], preferred_element_type=F32).astype(BF16)
    nt = (((1,), (1,)), ((), ()))
    s1 = lax.dot_general(k1_ref[...], q[:, :PEER_HALF], nt, preferred_element_type=F32)
    s2 = lax.dot_general(k2_ref[...], q[:, PEER_HALF:], nt, preferred_element_type=F32)
    _top_rounds(s1, PEER_TOPK, v1, i1)
    _top_rounds(s2, PEER_TOPK, v2, i2)
    a2 = v2[...]
    j2 = i2[...]
    cand = jnp.concatenate([v1[a:a + 1, :] + a2 for a in range(PEER_TOPK)], axis=0)
    cidx = jnp.concatenate([i1[a:a + 1, :] * PEER_KEYS + j2 for a in range(PEER_TOPK)], axis=0)
    _top_rounds(cand, PEER_TOPK, vt, it, payload=cidx)
    top = vt[...]
    e = jnp.exp(top - top[0:1, :])
    gate_ref[...] = e / jnp.sum(e, axis=0, keepdims=True)
    idx_ref[...] = it[...]


def peer_route(x, wq, k1, k2, tt=256):
    t, d = x.shape
    tt = min(tt, t)
    nt = t // tt
    npair = PEER_PAIRS
    blk = pl.BlockSpec((None, PEER_TOPK, tt), lambda i, h: (i, h, 0))
    f32s = pltpu.VMEM((PEER_TOPK, tt), F32)
    i32s = pltpu.VMEM((PEER_TOPK, tt), jnp.int32)
    return pl.pallas_call(
        _peer_route_kernel,
        grid=(nt, PEER_HEADS),
        in_specs=[pl.BlockSpec((tt, d), lambda i, h: (i, 0)),
                  pl.BlockSpec((d, PEER_QDIM), lambda i, h: (0, h)),
                  pl.BlockSpec((PEER_KEYS, PEER_HALF), lambda i, h: (0, 0)),
                  pl.BlockSpec((PEER_KEYS, PEER_HALF), lambda i, h: (0, 0))],
        out_specs=[blk, blk],
        out_shape=[jax.ShapeDtypeStruct((nt, npair, tt), jnp.int32),
                   jax.ShapeDtypeStruct((nt, npair, tt), F32)],
        scratch_shapes=[f32s, i32s, f32s, i32s, f32s, i32s],
        compiler_params=_params("parallel", "arbitrary"),
        name="peer_route",
    )(x, wq, k1, k2)


PEER_GROUP = SUBLANES


def _peer_mix_kernel(idx_ref, gate_ref, x_ref, tab_ref, o_ref, buf, sem, *, tt):
    d = D_MODEL
    ngroup = tt // PEER_GROUP

    def row_copy(tok, slot, r):
        e = idx_ref[tok, r]
        return pltpu.make_async_copy(tab_ref.at[pl.ds(e, 1), :], buf.at[slot, pl.ds(r, 1), :], sem.at[slot])

    def issue(g, half):
        for j in range(PEER_GROUP):
            tok = g * PEER_GROUP + j
            for r in range(PEER_PAIRS):
                row_copy(tok, half * PEER_GROUP + j, r).start()

    def wait(half):
        for j in range(PEER_GROUP):
            slot = half * PEER_GROUP + j
            pltpu.make_async_copy(tab_ref.at[pl.ds(0, PEER_PAIRS), :], buf.at[slot], sem.at[slot]).wait()

    def compute(g, half):
        tok0 = pl.multiple_of(g * PEER_GROUP, PEER_GROUP)
        x8 = x_ref[pl.ds(tok0, PEER_GROUP), :].astype(BF16)
        rowid = lax.broadcasted_iota(jnp.int32, (PEER_GROUP, PEER_PAIRS), 0)
        act = jnp.zeros((PEER_GROUP, PEER_PAIRS), F32)
        for j in range(PEER_GROUP):
            u = buf[half * PEER_GROUP + j, :, 0:d].astype(BF16)
            a = lax.dot_general(x8, u, (((1,), (1,)), ((), ())), preferred_element_type=F32)
            act = jnp.where(rowid == j, a, act)
        gelu = 0.5 * act * (1.0 + lax.erf(act * (2.0 ** -0.5)))
        coef = gate_ref[pl.ds(tok0, PEER_GROUP), :] * gelu
        cb = coef.astype(BF16)
        orow = lax.broadcasted_iota(jnp.int32, (PEER_GROUP, d), 0)
        out = jnp.zeros((PEER_GROUP, d), F32)
        for j in range(PEER_GROUP):
            v = buf[half * PEER_GROUP + j, :, d:2 * d].astype(BF16)
            o = jnp.dot(cb, v, preferred_element_type=F32)
            out = jnp.where(orow == j, o, out)
        o_ref[pl.ds(tok0, PEER_GROUP), :] = out

    issue(0, 0)

    def body(gg, carry):
        for half in range(2):
            g = gg * 2 + half

            @pl.when(g + 1 < ngroup)
            def _():
                issue(g + 1, 1 - half)

            wait(half)
            compute(g, half)
        return carry

    lax.fori_loop(0, ngroup // 2, body, 0)


def peer_mix(idx, gate, x, tab, tt=128):
    t, d = x.shape
    tt = min(tt, t)
    row = lambda i: (i, 0)
    return pl.pallas_call(
        functools.partial(_peer_mix_kernel, tt=tt),
        grid=(t // tt,),
        in_specs=[pl.BlockSpec((tt, PEER_PAIRS), row, memory_space=pltpu.SMEM),
                  pl.BlockSpec((tt, PEER_PAIRS), row),
                  pl.BlockSpec((tt, d), row),
                  pl.BlockSpec(memory_space=pl.ANY)],
        out_specs=pl.BlockSpec((tt, d), row),
        out_shape=jax.ShapeDtypeStruct((t, d), F32),
        scratch_shapes=[pltpu.VMEM((2 * PEER_GROUP, PEER_PAIRS, 2 * d), F32),
                        pltpu.SemaphoreType.DMA((2 * PEER_GROUP,))],
        compiler_params=_params("arbitrary"),
        name="peer_mix",
    )(idx, gate, x, tab)


def peer_ffn(xf, w, i):
    idx_t, gate_t = peer_route(xf, w["peer_w_q"][i].astype(BF16), w["peer_k1"][i].astype(BF16),
                               w["peer_k2"][i].astype(BF16))
    t = xf.shape[0]
    idx = idx_t.transpose(0, 2, 1).reshape(t, PEER_PAIRS)
    gate = gate_t.transpose(0, 2, 1).reshape(t, PEER_PAIRS)
    tab = jnp.concatenate([w["peer_u"][i], w["peer_v"][i]], axis=1)
    return peer_mix(idx, gate, xf, tab)


def trunk(x, p, w):
    bsz, seq, d = x.shape
    t = bsz * seq
    xf = x.reshape(t, d)
    pf = p.reshape(DEPTH, t, PLE_DIM)
    for i in range(DEPTH):
        j = i // 2
        if i % 2 == 0:
            mix = hyena_mixer(xf, bsz, seq, w, j)
            w_out, b_out = w["hy_w_out"][j], w["hy_b_out"][j]
        else:
            mix = na_mixer(xf, bsz, seq, w, j)
            w_out, b_out = w["na_w_out"][j], w["na_b_out"][j]
        xf = outproj_ln(mix, w_out.astype(BF16), b_out[None, :], xf, w["ln1_g"][i][None, :], w["ln1_b"][i][None, :])
        ffn = peer_ffn(xf, w, i)
        xf = ln_ple(xf, ffn, pf[i], w["ple_gate_w"][i].astype(BF16), w["ple_w"][i].astype(BF16),
                    w["ln2_g"][i][None, :], w["ln2_b"][i][None, :])
    return xf.reshape(bsz, seq, d)


def kernel(x_prompt, x_sample, p_prompt, p_sample, hy_w_in, hy_b_in, hy_w_conv, hy_b_conv, hy_f_w1, hy_f_b1,
           hy_f_w2, hy_f_b2, hy_f_w3, hy_decay, hy_skip, hy_w_out, hy_b_out, na_w_qkv, na_b_qkv, na_rpb,
           na_w_out, na_b_out, ln1_g, ln1_b, ln2_g, ln2_b, peer_w_q, peer_k1, peer_k2, peer_u, peer_v,
           ple_w, ple_gate_w):
    w = dict(hy_w_in=hy_w_in, hy_b_in=hy_b_in, hy_w_conv=hy_w_conv, hy_b_conv=hy_b_conv,
             hy_f_w1=hy_f_w1, hy_f_b1=hy_f_b1, hy_f_w2=hy_f_w2, hy_f_b2=hy_f_b2, hy_f_w3=hy_f_w3,
             hy_decay=hy_decay, hy_skip=hy_skip, hy_w_out=hy_w_out, hy_b_out=hy_b_out,
             na_w_qkv=na_w_qkv, na_b_qkv=na_b_qkv, na_rpb=na_rpb, na_w_out=na_w_out, na_b_out=na_b_out,
             ln1_g=ln1_g, ln1_b=ln1_b, ln2_g=ln2_g, ln2_b=ln2_b,
             peer_w_q=peer_w_q, peer_k1=peer_k1, peer_k2=peer_k2, peer_u=peer_u, peer_v=peer_v,
             ple_w=ple_w, ple_gate_w=ple_gate_w)
    return (trunk(x_prompt, p_prompt, w), trunk(x_sample, p_sample, w))
```

```python
import functools
import math

import jax
import jax.numpy as jnp
from jax import lax
from jax.experimental import pallas as pl
from jax.experimental.pallas import tpu as pltpu

F32 = jnp.float32
BF16 = jnp.bfloat16

D_MODEL = 1024
DEPTH = 2
DEEPNORM_ALPHA = (2.0 * DEPTH) ** 0.25
LN_EPS = 1e-5
NEG_INF = -1e30
PLE_DIM = 256

GRID_W = 64
HY_BANDS = 16
HY_EMB = 1 + 2 * HY_BANDS
HY_HIDDEN = 64
HY_FILTERS = 4

NA_HEADS = 16
NA_HEAD_DIM = D_MODEL // NA_HEADS
WIN_R = 8
WIN_C = 16
NA_HEAD_GROUP = 4

PEER_HEADS = 8
PEER_KEYS = 128
PEER_QDIM = 256
PEER_HALF = PEER_QDIM // 2
PEER_TOPK = 16
PEER_PAIRS = PEER_HEADS * PEER_TOPK

LANES = 128
SUBLANES = 8
FFT_N2 = 128
VMEM_LIMIT = 56 * 1024 * 1024


def _params(*sem):
    return pltpu.CompilerParams(dimension_semantics=sem, vmem_limit_bytes=VMEM_LIMIT)


def _layer_norm(x, g, b):
    mu = jnp.mean(x, axis=-1, keepdims=True)
    xc = x - mu
    var = jnp.mean(xc * xc, axis=-1, keepdims=True)
    return xc * lax.rsqrt(var + LN_EPS) * g + b


def _mm_bias_kernel(x_ref, w_ref, b_ref, s_ref, o_ref):
    acc = jnp.dot(x_ref[...].astype(BF16), w_ref[...], preferred_element_type=F32)
    o_ref[...] = ((acc + b_ref[...]) * s_ref[...]).astype(o_ref.dtype)


def mm_bias(x, w, b, colscale, out_dtype, tm=512):
    t, k = x.shape
    n = w.shape[1]
    tm = min(tm, t)
    return pl.pallas_call(
        _mm_bias_kernel,
        grid=(t // tm,),
        in_specs=[pl.BlockSpec((tm, k), lambda i: (i, 0)),
                  pl.BlockSpec((k, n), lambda i: (0, 0)),
                  pl.BlockSpec((1, n), lambda i: (0, 0)),
                  pl.BlockSpec((1, n), lambda i: (0, 0))],
        out_specs=pl.BlockSpec((tm, n), lambda i: (i, 0)),
        out_shape=jax.ShapeDtypeStruct((t, n), out_dtype),
        compiler_params=_params("parallel"),
        name="mm_bias",
    )(x, w, b, colscale)


def _outproj_ln_kernel(a_ref, w_ref, b_ref, x_ref, g_ref, beta_ref, o_ref):
    mix = jnp.dot(a_ref[...].astype(BF16), w_ref[...], preferred_element_type=F32) + b_ref[...]
    o_ref[...] = _layer_norm(DEEPNORM_ALPHA * x_ref[...] + mix, g_ref[...], beta_ref[...])


def outproj_ln(a, w, b, x, g, beta, tm=512):
    t, d = x.shape
    tm = min(tm, t)
    row = lambda i: (i, 0)
    fix = lambda i: (0, 0)
    return pl.pallas_call(
        _outproj_ln_kernel,
        grid=(t // tm,),
        in_specs=[pl.BlockSpec((tm, d), row), pl.BlockSpec((d, d), fix), pl.BlockSpec((1, d), fix),
                  pl.BlockSpec((tm, d), row), pl.BlockSpec((1, d), fix), pl.BlockSpec((1, d), fix)],
        out_specs=pl.BlockSpec((tm, d), row),
        out_shape=jax.ShapeDtypeStruct((t, d), F32),
        compiler_params=_params("parallel"),
        name="outproj_ln",
    )(a, w, b, x, g, beta)


def _ln_ple_kernel(x_ref, f_ref, p_ref, wg_ref, wp_ref, g_ref, beta_ref, o_ref):
    x2 = _layer_norm(DEEPNORM_ALPHA * x_ref[...] + f_ref[...], g_ref[...], beta_ref[...])
    gate = jax.nn.sigmoid(jnp.dot(x2.astype(BF16), wg_ref[...], preferred_element_type=F32))
    emb = jnp.dot(p_ref[...].astype(BF16), wp_ref[...], preferred_element_type=F32)
    o_ref[...] = x2 + gate * emb


def ln_ple(x, ffn, p, wg, wp, g, beta, tm=512):
    t, d = x.shape
    tm = min(tm, t)
    row = lambda i: (i, 0)
    fix = lambda i: (0, 0)
    return pl.pallas_call(
        _ln_ple_kernel,
        grid=(t // tm,),
        in_specs=[pl.BlockSpec((tm, d), row), pl.BlockSpec((tm, d), row), pl.BlockSpec((tm, PLE_DIM), row),
                  pl.BlockSpec((d, d), fix), pl.BlockSpec((PLE_DIM, d), fix),
                  pl.BlockSpec((1, d), fix), pl.BlockSpec((1, d), fix)],
        out_specs=pl.BlockSpec((tm, d), row),
        out_shape=jax.ShapeDtypeStruct((t, d), F32),
        compiler_params=_params("parallel"),
        name="ln_ple",
    )(x, ffn, p, wg, wp, g, beta)


def _hy_inproj_kernel(x_ref, xp_ref, xn_ref, w_ref, b_ref, wc_ref, bc_ref, o_ref, *, tm, seq):
    i = pl.program_id(1)
    w = w_ref[...]
    b = b_ref[...]
    u = jnp.dot(x_ref[...].astype(BF16), w, preferred_element_type=F32) + b
    up = jnp.dot(xp_ref[...].astype(BF16), w, preferred_element_type=F32) + b
    un = jnp.dot(xn_ref[...].astype(BF16), w, preferred_element_type=F32) + b
    first = (i * tm) % seq == 0
    last = ((i + 1) * tm) % seq == 0
    prev_row = jnp.where(first, 0.0, up[SUBLANES - 1:SUBLANES, :])
    next_row = jnp.where(last, 0.0, un[0:1, :])
    rows = lax.broadcasted_iota(jnp.int32, u.shape, 0)
    um1 = jnp.where(rows == 0, prev_row, pltpu.roll(u, 1, 0))
    up1 = jnp.where(rows == tm - 1, next_row, pltpu.roll(u, tm - 1, 0))
    wc = wc_ref[...]
    o_ref[...] = um1 * wc[0:1, :] + u * wc[1:2, :] + up1 * wc[2:3, :] + bc_ref[...]


def hy_inproj(x, w, b, wc, bc, seq, tm=512):
    t, d = x.shape
    tm = min(tm, seq)
    nblk8 = t // SUBLANES
    per = tm // SUBLANES
    kern = functools.partial(_hy_inproj_kernel, tm=tm, seq=seq)
    return pl.pallas_call(
        kern,
        grid=(3, t // tm),
        in_specs=[pl.BlockSpec((tm, d), lambda j, i: (i, 0)),
                  pl.BlockSpec((SUBLANES, d), lambda j, i: (jnp.maximum(i * per - 1, 0), 0)),
                  pl.BlockSpec((SUBLANES, d), lambda j, i: (jnp.minimum((i + 1) * per, nblk8 - 1), 0)),
                  pl.BlockSpec((d, d), lambda j, i: (0, j)),
                  pl.BlockSpec((1, d), lambda j, i: (0, j)),
                  pl.BlockSpec((3, d), lambda j, i: (0, j)),
                  pl.BlockSpec((1, d), lambda j, i: (0, j))],
        out_specs=pl.BlockSpec((None, tm, d), lambda j, i: (j, i, 0)),
        out_shape=jax.ShapeDtypeStruct((3, t, d), F32),
        compiler_params=_params("parallel", "parallel"),
        name="hy_inproj",
    )(x, x, x, w, b, wc, bc)


def _hy_filter_kernel(feat_ref, w1_ref, b1_ref, w2_ref, b2_ref, w3_ref, dec_ref, o_ref, *, tl):
    hi = lax.Precision.HIGHEST
    feat = feat_ref[...]
    h = jnp.sin(jnp.dot(feat, w1_ref[...], preferred_element_type=F32, precision=hi) + b1_ref[...])
    h = jnp.sin(jnp.dot(h, w2_ref[...], preferred_element_type=F32, precision=hi) + b2_ref[...])
    h = jnp.dot(h, w3_ref[...], preferred_element_type=F32, precision=hi)
    h = h * jnp.exp(-feat[:, 0:1] * jnp.abs(dec_ref[...]))
    pos = pl.program_id(0) * tl + lax.broadcasted_iota(jnp.int32, (tl, D_MODEL), 0)
    for f in range(HY_FILTERS):
        piece = h[:, f * D_MODEL:(f + 1) * D_MODEL]
        if f >= HY_FILTERS // 2:
            piece = jnp.where(pos == 0, 0.0, piece)
        o_ref[f] = piece


def hy_filter(seq, w1, b1, w2, b2, w3, decay, tl=256):
    tl = min(tl, seq)
    t_norm = jnp.linspace(0.0, 1.0, seq, dtype=F32)[:, None]
    t_idx = jnp.arange(seq, dtype=F32)[:, None]
    bands = jnp.linspace(1e-4, HY_BANDS - 1, HY_BANDS, dtype=F32)[None, :]
    phase = bands * t_idx * (2.0 * math.pi / seq)
    feat = jnp.concatenate([t_norm, jnp.cos(phase), -jnp.sin(phase)], axis=-1)
    feat = jnp.pad(feat, ((0, 0), (0, LANES - HY_EMB)))
    padh = LANES - HY_HIDDEN
    w1p = jnp.pad(w1, ((0, LANES - HY_EMB), (0, padh)))
    b1p = jnp.pad(b1, (0, padh))[None, :]
    w2p = jnp.pad(w2, ((0, padh), (0, padh)))
    b2p = jnp.pad(b2, (0, padh))[None, :]
    w3p = jnp.pad(w3, ((0, padh), (0, 0)))
    nf = HY_FILTERS * D_MODEL
    fix = lambda i: (0, 0)
    return pl.pallas_call(
        functools.partial(_hy_filter_kernel, tl=tl),
        grid=(seq // tl,),
        in_specs=[pl.BlockSpec((tl, LANES), lambda i: (i, 0)),
                  pl.BlockSpec((LANES, LANES), fix), pl.BlockSpec((1, LANES), fix),
                  pl.BlockSpec((LANES, LANES), fix), pl.BlockSpec((1, LANES), fix),
                  pl.BlockSpec((LANES, nf), fix), pl.BlockSpec((1, nf), fix)],
        out_specs=pl.BlockSpec((HY_FILTERS, tl, D_MODEL), lambda i: (0, i, 0)),
        out_shape=jax.ShapeDtypeStruct((HY_FILTERS, seq, D_MODEL), F32),
        compiler_params=_params("parallel"),
        name="hy_filter",
    )(feat, w1p, b1p, w2p, b2p, w3p, decay.reshape(1, nf))


def _dft_tables(seq):
    n = 2 * seq
    n2 = FFT_N2
    n1 = n // n2
    n1h = n1 // 2
    two_pi = 2.0 * math.pi

    def ang(m, period):
        return (m % period).astype(F32) * (two_pi / period)

    i_n2 = jnp.arange(n2, dtype=jnp.int32)[:, None, None]
    i_k1 = jnp.arange(n1, dtype=jnp.int32)[None, :, None]
    i_n1 = jnp.arange(n1h, dtype=jnp.int32)[None, None, :]
    th = ang(i_k1 * (n2 * i_n1 + i_n2), n)
    fa = jnp.concatenate([jnp.cos(th), -jnp.sin(th)], axis=1)
    a = jnp.arange(n2, dtype=jnp.int32)
    ph = ang(a[:, None] * a[None, :], n2)
    c, s = jnp.cos(ph), jnp.sin(ph)
    mf = jnp.concatenate([jnp.concatenate([c, s], axis=1), jnp.concatenate([-s, c], axis=1)], axis=0)
    j_k1 = jnp.arange(n1, dtype=jnp.int32)[:, None, None]
    j_n2 = jnp.arange(n2, dtype=jnp.int32)[None, :, None]
    j_k2 = jnp.arange(n2, dtype=jnp.int32)[None, None, :]
    phi = ang(j_n2 * (j_k1 + n1 * j_k2), n)
    c, s = jnp.cos(phi), jnp.sin(phi)
    gi = jnp.concatenate([jnp.concatenate([c, -s], axis=2), jnp.concatenate([s, c], axis=2)], axis=1)
    r_n1 = jnp.arange(n1h, dtype=jnp.int32)[:, None]
    r_k1 = jnp.arange(n1, dtype=jnp.int32)[None, :]
    psi = ang(r_n1 * r_k1, n1)
    fb = jnp.concatenate([jnp.cos(psi), -jnp.sin(psi)], axis=1) * (1.0 / n)
    return dict(n1=n1, n2=n2, n1h=n1h, fa=fa, mf=mf, gi=gi, fb=fb)


def _dot(a, b, exact):
    if exact:
        return jnp.dot(a, b, preferred_element_type=F32, precision=lax.Precision.HIGHEST)
    return jnp.dot(a, b.astype(BF16), preferred_element_type=F32)


def _fft_a_kernel(x_ref, fa_ref, o_ref, *, n1, exact):
    for j in range(SUBLANES):
        r = _dot(fa_ref[j], x_ref[:, j, :], exact)
        o_ref[0, :, j, :] = r[:n1]
        o_ref[1, :, j, :] = r[n1:]


def fft_stage_a(x, tab, exact=False):
    bz, seq, d = x.shape
    n1, n2, n1h = tab["n1"], tab["n2"], tab["n1h"]
    g8 = n2 // SUBLANES
    fa = tab["fa"] if exact else tab["fa"].astype(BF16)
    x5 = x.reshape(bz, n1h, g8, SUBLANES, d)
    out = pl.pallas_call(
        functools.partial(_fft_a_kernel, n1=n1, exact=exact),
        grid=(bz, g8),
        in_specs=[pl.BlockSpec((None, n1h, None, SUBLANES, d), lambda b, g: (b, 0, g, 0, 0)),
                  pl.BlockSpec((SUBLANES, 2 * n1, n1h), lambda b, g: (g, 0, 0))],
        out_specs=pl.BlockSpec((None, 2, n1, None, SUBLANES, d), lambda b, g: (b, 0, 0, g, 0, 0)),
        out_shape=jax.ShapeDtypeStruct((bz, 2, n1, g8, SUBLANES, d), F32),
        compiler_params=_params("parallel", "parallel"),
        name="fft_stage_a",
    )(x5, fa)
    return out.reshape(bz, 2, n1, n2, d)


def _filter_spec_kernel(cf_ref, cb_ref, mf_ref, o_ref, *, n2):
    mf = mf_ref[...]
    yf = _dot(mf, jnp.concatenate([cf_ref[0], cf_ref[1]], axis=0), True)
    yb = _dot(mf, jnp.concatenate([cb_ref[0], cb_ref[1]], axis=0), True)
    o_ref[0] = yf[:n2] + yb[:n2]
    o_ref[1] = yf[n2:] - yb[n2:]


def filter_spectrum(cf, tab):
    n1, n2 = tab["n1"], tab["n2"]
    d = cf.shape[-1]
    norder = HY_FILTERS // 2
    blk = (None, 2, None, n2, d)
    return pl.pallas_call(
        functools.partial(_filter_spec_kernel, n2=n2),
        grid=(norder, n1),
        in_specs=[pl.BlockSpec(blk, lambda o, k: (o, 0, k, 0, 0)),
                  pl.BlockSpec(blk, lambda o, k: (norder + o, 0, k, 0, 0)),
                  pl.BlockSpec((2 * n2, 2 * n2), lambda o, k: (0, 0))],
        out_specs=pl.BlockSpec(blk, lambda o, k: (o, 0, k, 0, 0)),
        out_shape=jax.ShapeDtypeStruct((norder, 2, n1, n2, d), F32),
        compiler_params=_params("parallel", "parallel"),
        name="filter_spectrum",
    )(cf, cf, tab["mf"])


def _fft_mid_kernel(c_ref, kf_ref, mf_ref, gi_ref, o_ref, *, n2):
    c = jnp.concatenate([c_ref[0], c_ref[1]], axis=0)
    y = _dot(mf_ref[...], c, False)
    yr, yi = y[:n2], y[n2:]
    kr, ki = kf_ref[0], kf_ref[1]
    z = jnp.concatenate([yr * kr - yi * ki, yr * ki + yi * kr], axis=0)
    dd = _dot(gi_ref[...], z, False)
    o_ref[0] = dd[:n2]
    o_ref[1] = dd[n2:]


def fft_mid(c, kf, tab):
    bz, _, n1, n2, d = c.shape
    blk = (None, 2, None, n2, d)
    return pl.pallas_call(
        functools.partial(_fft_mid_kernel, n2=n2),
        grid=(n1, bz),
        in_specs=[pl.BlockSpec(blk, lambda k, b: (b, 0, k, 0, 0)),
                  pl.BlockSpec((2, None, n2, d), lambda k, b: (0, k, 0, 0)),
                  pl.BlockSpec((2 * n2, 2 * n2), lambda k, b: (0, 0)),
                  pl.BlockSpec((None, 2 * n2, 2 * n2), lambda k, b: (k, 0, 0))],
        out_specs=pl.BlockSpec(blk, lambda k, b: (b, 0, k, 0, 0)),
        out_shape=jax.ShapeDtypeStruct((bz, 2, n1, n2, d), F32),
        compiler_params=_params("parallel", "parallel"),
        name="fft_mid",
    )(c, kf, tab["mf"].astype(BF16), tab["gi"].astype(BF16))


def _fft_b_kernel(d_ref, fb_ref, z_ref, g_ref, skip_ref, o_ref):
    fb = fb_ref[...]
    skip = skip_ref[...]
    for j in range(SUBLANES):
        dd = jnp.concatenate([d_ref[0, :, j, :], d_ref[1, :, j, :]], axis=0)
        y = _dot(fb, dd, False)
        o_ref[:, j, :] = (y + skip * z_ref[:, j, :]) * g_ref[:, j, :]


def fft_stage_b(dp, tab, z, gate, skip):
    bz, _, n1, n2, d = dp.shape
    n1h = tab["n1h"]
    g8 = n2 // SUBLANES
    seq = z.shape[1]
    dp6 = dp.reshape(bz, 2, n1, g8, SUBLANES, d)
    view = lambda a: a.reshape(bz, n1h, g8, SUBLANES, d)
    tok = pl.BlockSpec((None, n1h, None, SUBLANES, d), lambda b, g: (b, 0, g, 0, 0))
    out = pl.pallas_call(
        _fft_b_kernel,
        grid=(bz, g8),
        in_specs=[pl.BlockSpec((None, 2, n1, None, SUBLANES, d), lambda b, g: (b, 0, 0, g, 0, 0)),
                  pl.BlockSpec((n1h, 2 * n1), lambda b, g: (0, 0)),
                  tok, tok,
                  pl.BlockSpec((1, d), lambda b, g: (0, 0))],
        out_specs=tok,
        out_shape=jax.ShapeDtypeStruct((bz, n1h, g8, SUBLANES, d), F32),
        compiler_params=_params("parallel", "parallel"),
        name="fft_stage_b",
    )(dp6, tab["fb"].astype(BF16), view(z), view(gate), skip)
    return out.reshape(bz, seq, d)


def hyena_mixer(xf, bsz, seq, w, j):
    d = D_MODEL
    uvg = hy_inproj(xf, w["hy_w_in"][j].astype(BF16), w["hy_b_in"][j][None, :],
                    w["hy_w_conv"][j], w["hy_b_conv"][j][None, :], seq)
    uvg = uvg.reshape(3, bsz, seq, d)
    tab = _dft_tables(seq)
    hk = hy_filter(seq, w["hy_f_w1"][j], w["hy_f_b1"][j], w["hy_f_w2"][j], w["hy_f_b2"][j],
                   w["hy_f_w3"][j], w["hy_decay"][j])
    kf = filter_spectrum(fft_stage_a(hk, tab, exact=True), tab)
    z = uvg[0]
    for order in range(2):
        c = fft_stage_a(z, tab)
        dp = fft_mid(c, kf[order], tab)
        z = fft_stage_b(dp, tab, z, uvg[1 + order], w["hy_skip"][j][order][None, :])
    return z.reshape(bsz * seq, d)


def _na_bias_kernel(rpb_ref, o_ref):
    h = pl.program_id(0)
    q = lax.broadcasted_iota(jnp.int32, (GRID_W, GRID_W), 0)
    kc = lax.broadcasted_iota(jnp.int32, (GRID_W, GRID_W), 1)
    win_start = jnp.clip(q - WIN_C // 2, 0, GRID_W - WIN_C)
    valid = (kc >= win_start) & (kc < win_start + WIN_C)
    dc = jnp.clip(kc - q + WIN_C - 1, 0, 2 * WIN_C - 2)
    nrow = 2 * WIN_R - 1
    for dr in range(nrow):
        t = jnp.zeros((GRID_W, GRID_W), F32)
        for c in range(2 * WIN_C - 1):
            t = jnp.where(dc == c, rpb_ref[h * nrow + dr, c], t)
        o_ref[dr] = jnp.where(valid, t, NEG_INF)


def na_bias_table(rpb):
    nrow = 2 * WIN_R - 1
    ncol = 2 * WIN_C - 1
    return pl.pallas_call(
        _na_bias_kernel,
        grid=(NA_HEADS,),
        in_specs=[pl.BlockSpec(memory_space=pltpu.SMEM)],
        out_specs=pl.BlockSpec((None, nrow, GRID_W, GRID_W), lambda h: (h, 0, 0, 0)),
        out_shape=jax.ShapeDtypeStruct((NA_HEADS, nrow, GRID_W, GRID_W), F32),
        compiler_params=_params("parallel"),
        name="na_bias",
    )(rpb.reshape(NA_HEADS * nrow, ncol))


def _na_attn_kernel(q_ref, k_ref, v_ref, bias_ref, o_ref, *, rows):
    r = pl.program_id(2)
    rs = jnp.clip(r - WIN_R // 2, 0, rows - WIN_R)
    start = pl.multiple_of(rs * GRID_W, GRID_W)
    nk = WIN_R * GRID_W
    kb = k_ref[pl.ds(start, nk), :]
    vb = v_ref[pl.ds(start, nk), :]
    q = q_ref[...]
    outs = []
    for hh in range(NA_HEAD_GROUP):
        sl = slice(hh * NA_HEAD_DIM, (hh + 1) * NA_HEAD_DIM)
        s = lax.dot_general(q[:, sl], kb[:, sl], (((1,), (1,)), ((), ())), preferred_element_type=F32)
        s = s + bias_ref[hh]
        m = jnp.max(s, axis=-1, keepdims=True)
        e = jnp.exp(s - m)
        p = e / jnp.sum(e, axis=-1, keepdims=True)
        outs.append(jnp.dot(p.astype(BF16), vb[:, sl], preferred_element_type=F32))
    o_ref[...] = jnp.concatenate(outs, axis=1)


def na_attention(qkv, bias, bsz, seq):
    rows = seq // GRID_W
    t = bsz * seq
    gl = NA_HEAD_GROUP * NA_HEAD_DIM
    ngrp = NA_HEADS // NA_HEAD_GROUP

    def offset(r):
        return r - jnp.clip(r - WIN_R // 2, 0, rows - WIN_R)

    return pl.pallas_call(
        functools.partial(_na_attn_kernel, rows=rows),
        grid=(bsz, ngrp, rows),
        in_specs=[pl.BlockSpec((GRID_W, gl), lambda b, g, r: (b * rows + r, g)),
                  pl.BlockSpec((seq, gl), lambda b, g, r: (b, ngrp + g)),
                  pl.BlockSpec((seq, gl), lambda b, g, r: (b, 2 * ngrp + g)),
                  pl.BlockSpec((NA_HEAD_GROUP, None, GRID_W, WIN_R * GRID_W), lambda b, g, r: (g, offset(r), 0, 0))],
        out_specs=pl.BlockSpec((GRID_W, gl), lambda b, g, r: (b * rows + r, g)),
        out_shape=jax.ShapeDtypeStruct((t, D_MODEL), F32),
        compiler_params=_params("parallel", "parallel", "arbitrary"),
        name="na_attention",
    )(qkv, qkv, qkv, bias)


def na_mixer(xf, bsz, seq, w, j):
    d = D_MODEL
    colscale = jnp.concatenate([jnp.full((1, d), NA_HEAD_DIM ** -0.5, F32), jnp.ones((1, 2 * d), F32)], axis=1)
    qkv = mm_bias(xf, w["na_w_qkv"][j].astype(BF16), w["na_b_qkv"][j][None, :], colscale, BF16)
    tb = na_bias_table(w["na_rpb"][j])
    dr = jnp.arange(WIN_R)[None, :] - jnp.arange(WIN_R)[:, None] + WIN_R - 1
    bias = tb[:, dr]
    bias = bias.transpose(0, 1, 3, 2, 4).reshape(NA_HEADS, WIN_R, GRID_W, WIN_R * GRID_W)
    return na_attention(qkv, bias, bsz, seq)


def _top_rounds(s, n_round, val_ref, idx_ref, key=None, payload=None):
    nrow, ncol = s.shape
    if key is None:
        key = lax.broadcasted_iota(jnp.int32, (nrow, ncol), 0)
    big = jnp.iinfo(jnp.int32).max

    def body(r, s):
        m = jnp.max(s, axis=0, keepdims=True)
        am = jnp.min(jnp.where(s == m, key, big), axis=0, keepdims=True)
        hit = key == am
        val_ref[pl.ds(r, 1), :] = m
        if payload is None:
            idx_ref[pl.ds(r, 1), :] = am
        else:
            idx_ref[pl.ds(r, 1), :] = jnp.max(jnp.where(hit, payload, -1), axis=0, keepdims=True)
        return jnp.where(hit, -jnp.inf, s)

    lax.fori_loop(0, n_round, body, s)


def _pair_candidates(v1, i1, v2, i2):
    tt = v1.shape[1]
    sub = lax.broadcasted_iota(jnp.int32, (SUBLANES, tt), 0)
    vals, keys, ids = [], [], []
    for a in range(PEER_TOPK // 2):
        nb = PEER_TOPK // (a + 1)
        va = v1[a:a + 1, :]
        ia = i1[a:a + 1, :] * PEER_KEYS
        for b0 in range(0, nb, SUBLANES):
            val = va + v2[b0:b0 + SUBLANES, :]
            if nb - b0 < SUBLANES:
                val = jnp.where(sub < nb - b0, val, -jnp.inf)
            vals.append(val)
            keys.append(a * PEER_TOPK + b0 + sub)
            ids.append(ia + i2[b0:b0 + SUBLANES, :])
    half = PEER_TOPK // 2
    vals.append(v1[half:, :] + v2[0:1, :])
    keys.append((half + sub) * PEER_TOPK)
    ids.append(i1[half:, :] * PEER_KEYS + i2[0:1, :])
    cat = lambda xs: jnp.concatenate(xs, axis=0)
    return cat(vals), cat(keys), cat(ids)


def _peer_route_kernel(x_ref, wq_ref, k1_ref, k2_ref, idx_ref, gate_ref, v1, i1, v2, i2, vt, it):
    q = jnp.dot(x_ref[...].astype(BF16), wq_ref[...], preferred_element_type=F32).astype(BF16)
    nt = (((1,), (1,)), ((), ()))
    s1 = lax.dot_general(k1_ref[...], q[:, :PEER_HALF], nt, preferred_element_type=F32)
    s2 = lax.dot_general(k2_ref[...], q[:, PEER_HALF:], nt, preferred_element_type=F32)
    _top_rounds(s1, PEER_TOPK, v1, i1)
    _top_rounds(s2, PEER_TOPK, v2, i2)
    cand, ckey, cidx = _pair_candidates(v1, i1, v2, i2)
    _top_rounds(cand, PEER_TOPK, vt, it, key=ckey, payload=cidx)
    top = vt[...]
    e = jnp.exp(top - top[0:1, :])
    gate_ref[...] = e / jnp.sum(e, axis=0, keepdims=True)
    idx_ref[...] = it[...]


def peer_route(x, wq, k1, k2, tt=256):
    t, d = x.shape
    tt = min(tt, t)
    nt = t // tt
    npair = PEER_PAIRS
    blk = pl.BlockSpec((None, PEER_TOPK, tt), lambda i, h: (i, h, 0))
    f32s = pltpu.VMEM((PEER_TOPK, tt), F32)
    i32s = pltpu.VMEM((PEER_TOPK, tt), jnp.int32)
    return pl.pallas_call(
        _peer_route_kernel,
        grid=(nt, PEER_HEADS),
        in_specs=[pl.BlockSpec((tt, d), lambda i, h: (i, 0)),
                  pl.BlockSpec((d, PEER_QDIM), lambda i, h: (0, h)),
                  pl.BlockSpec((PEER_KEYS, PEER_HALF), lambda i, h: (0, 0)),
                  pl.BlockSpec((PEER_KEYS, PEER_HALF), lambda i, h: (0, 0))],
        out_specs=[blk, blk],
        out_shape=[jax.ShapeDtypeStruct((nt, npair, tt), jnp.int32),
                   jax.ShapeDtypeStruct((nt, npair, tt), F32)],
        scratch_shapes=[f32s, i32s, f32s, i32s, f32s, i32s],
        compiler_params=_params("parallel", "arbitrary"),
        name="peer_route",
    )(x, wq, k1, k2)


PEER_GROUP = SUBLANES


PEER_SETS = 3
PEER_AHEAD = PEER_SETS - 1


def _peer_mix_kernel(idx_ref, idxn_ref, gate_ref, x_ref, tab_ref, o_ref, buf, sem, *, tt, ntile):
    d = D_MODEL
    ngroup = tt // PEER_GROUP
    half_rows = PEER_PAIRS // 2
    tile = pl.program_id(0)

    def set_base(g):
        return lax.rem(tile * ngroup + g, PEER_SETS) * PEER_GROUP

    def issue(iref, tok, slot, r0, r1):
        for r in range(r0, r1):
            e = iref[tok, r]
            cp = pltpu.make_async_copy(tab_ref.at[e], buf.at[slot, pl.ds(r, 1), :], sem.at[slot])
            cp.start(priority=r % 2)

    def wait(slot):
        pltpu.make_async_copy(tab_ref.at[pl.ds(0, PEER_PAIRS), 0, :], buf.at[slot], sem.at[slot]).wait()

    def step(g, iref, ptok0):
        base = set_base(g)
        nxt = set_base(g + PEER_AHEAD)
        for j in range(PEER_GROUP):
            wait(base + j)
        tok0 = pl.multiple_of(g * PEER_GROUP, PEER_GROUP)
        x8 = x_ref[pl.ds(tok0, PEER_GROUP), :].astype(BF16)
        rowid = lax.broadcasted_iota(jnp.int32, (PEER_GROUP, PEER_PAIRS), 0)
        act = jnp.zeros((PEER_GROUP, PEER_PAIRS), F32)
        for j in range(PEER_GROUP):
            issue(iref, ptok0 + j, nxt + j, 0, half_rows)
            u = buf[base + j, :, 0:d].astype(BF16)
            a = lax.dot_general(x8, u, (((1,), (1,)), ((), ())), preferred_element_type=F32)
            act = jnp.where(rowid == j, a, act)
        gelu = 0.5 * act * (1.0 + lax.erf(act * (2.0 ** -0.5)))
        cb = (gate_ref[pl.ds(tok0, PEER_GROUP), :] * gelu).astype(BF16)
        orow = lax.broadcasted_iota(jnp.int32, (PEER_GROUP, d), 0)
        out = jnp.zeros((PEER_GROUP, d), F32)
        for j in range(PEER_GROUP):
            issue(iref, ptok0 + j, nxt + j, half_rows, PEER_PAIRS)
            v = buf[base + j, :, d:2 * d].astype(BF16)
            o = jnp.dot(cb, v, preferred_element_type=F32)
            out = jnp.where(orow == j, o, out)
        o_ref[pl.ds(tok0, PEER_GROUP), :] = out

    @pl.when(tile == 0)
    def _():
        for g in range(PEER_AHEAD):
            for j in range(PEER_GROUP):
                issue(idx_ref, g * PEER_GROUP + j, g * PEER_GROUP + j, 0, PEER_PAIRS)

    def body(g, carry):
        step(g, idx_ref, (g + PEER_AHEAD) * PEER_GROUP)
        return carry

    lax.fori_loop(0, ngroup - PEER_AHEAD, body, 0)
    for k in range(PEER_AHEAD):
        step(ngroup - PEER_AHEAD + k, idxn_ref, k * PEER_GROUP)

    @pl.when(tile == ntile - 1)
    def _():
        for k in range(PEER_AHEAD):
            base = set_base(ngroup + k)
            for j in range(PEER_GROUP):
                wait(base + j)


def peer_mix(idx, gate, x, tab, tt=128):
    t, d = x.shape
    tt = min(tt, t)
    ntile = t // tt
    assert tt % PEER_GROUP == 0 and tt // PEER_GROUP > PEER_AHEAD
    row = lambda i: (i, 0)
    nxt = lambda i: (jnp.minimum(i + 1, ntile - 1), 0)
    return pl.pallas_call(
        functools.partial(_peer_mix_kernel, tt=tt, ntile=ntile),
        grid=(ntile,),
        in_specs=[pl.BlockSpec((tt, PEER_PAIRS), row, memory_space=pltpu.SMEM),
                  pl.BlockSpec((tt, PEER_PAIRS), nxt, memory_space=pltpu.SMEM),
                  pl.BlockSpec((tt, PEER_PAIRS), row),
                  pl.BlockSpec((tt, d), row),
                  pl.BlockSpec(memory_space=pl.ANY)],
        out_specs=pl.BlockSpec((tt, d), row),
        out_shape=jax.ShapeDtypeStruct((t, d), F32),
        scratch_shapes=[pltpu.VMEM((PEER_SETS * PEER_GROUP, PEER_PAIRS, 2 * d), F32),
                        pltpu.SemaphoreType.DMA((PEER_SETS * PEER_GROUP,))],
        compiler_params=_params("arbitrary"),
        name="peer_mix",
    )(idx, idx, gate, x, tab)


def peer_ffn(xf, w, i):
    idx_t, gate_t = peer_route(xf, w["peer_w_q"][i].astype(BF16), w["peer_k1"][i].astype(BF16),
                               w["peer_k2"][i].astype(BF16))
    t = xf.shape[0]
    idx = idx_t.transpose(0, 2, 1).reshape(t, PEER_PAIRS)
    gate = gate_t.transpose(0, 2, 1).reshape(t, PEER_PAIRS)
    tab = jnp.concatenate([w["peer_u"][i], w["peer_v"][i]], axis=1)[:, None, :]
    return peer_mix(idx, gate, xf, tab)


def trunk(x, p, w):
    bsz, seq, d = x.shape
    t = bsz * seq
    xf = x.reshape(t, d)
    pf = p.reshape(DEPTH, t, PLE_DIM)
    for i in range(DEPTH):
        j = i // 2
        if i % 2 == 0:
            mix = hyena_mixer(xf, bsz, seq, w, j)
            w_out, b_out = w["hy_w_out"][j], w["hy_b_out"][j]
        else:
            mix = na_mixer(xf, bsz, seq, w, j)
            w_out, b_out = w["na_w_out"][j], w["na_b_out"][j]
        xf = outproj_ln(mix, w_out.astype(BF16), b_out[None, :], xf, w["ln1_g"][i][None, :], w["ln1_b"][i][None, :])
        ffn = peer_ffn(xf, w, i)
        xf = ln_ple(xf, ffn, pf[i], w["ple_gate_w"][i].astype(BF16), w["ple_w"][i].astype(BF16),
                    w["ln2_g"][i][None, :], w["ln2_b"][i][None, :])
    return xf.reshape(bsz, seq, d)


def kernel(x_prompt, x_sample, p_prompt, p_sample, hy_w_in, hy_b_in, hy_w_conv, hy_b_conv, hy_f_w1, hy_f_b1,
           hy_f_w2, hy_f_b2, hy_f_w3, hy_decay, hy_skip, hy_w_out, hy_b_out, na_w_qkv, na_b_qkv, na_rpb,
           na_w_out, na_b_out, ln1_g, ln1_b, ln2_g, ln2_b, peer_w_q, peer_k1, peer_k2, peer_u, peer_v,
           ple_w, ple_gate_w):
    w = dict(hy_w_in=hy_w_in, hy_b_in=hy_b_in, hy_w_conv=hy_w_conv, hy_b_conv=hy_b_conv,
             hy_f_w1=hy_f_w1, hy_f_b1=hy_f_b1, hy_f_w2=hy_f_w2, hy_f_b2=hy_f_b2, hy_f_w3=hy_f_w3,
             hy_decay=hy_decay, hy_skip=hy_skip, hy_w_out=hy_w_out, hy_b_out=hy_b_out,
             na_w_qkv=na_w_qkv, na_b_qkv=na_b_qkv, na_rpb=na_rpb, na_w_out=na_w_out, na_b_out=na_b_out,
             ln1_g=ln1_g, ln1_b=ln1_b, ln2_g=ln2_g, ln2_b=ln2_b,
             peer_w_q=peer_w_q, peer_k1=peer_k1, peer_k2=peer_k2, peer_u=peer_u, peer_v=peer_v,
             ple_w=ple_w, ple_gate_w=ple_gate_w)
    return (trunk(x_prompt, p_prompt, w), trunk(x_sample, p_sample, w))
```

```python
import functools
import math

import jax
import jax.numpy as jnp
from jax import lax
from jax.experimental import pallas as pl
from jax.experimental.pallas import tpu as pltpu

F32 = jnp.float32
BF16 = jnp.bfloat16

D_MODEL = 1024
DEPTH = 2
DEEPNORM_ALPHA = (2.0 * DEPTH) ** 0.25
LN_EPS = 1e-5
NEG_INF = -1e30
PLE_DIM = 256

GRID_W = 64
HY_BANDS = 16
HY_EMB = 1 + 2 * HY_BANDS
HY_HIDDEN = 64
HY_FILTERS = 4

NA_HEADS = 16
NA_HEAD_DIM = D_MODEL // NA_HEADS
WIN_R = 8
WIN_C = 16
NA_HEAD_GROUP = 4
NA_ROWS = 2

PEER_HEADS = 8
PEER_KEYS = 128
PEER_QDIM = 256
PEER_HALF = PEER_QDIM // 2
PEER_TOPK = 16
PEER_PAIRS = PEER_HEADS * PEER_TOPK

LANES = 128
SUBLANES = 8
FFT_N2 = 128
VMEM_LIMIT = 56 * 1024 * 1024


def _params(*sem):
    return pltpu.CompilerParams(dimension_semantics=sem, vmem_limit_bytes=VMEM_LIMIT)


def _layer_norm(x, g, b):
    mu = jnp.mean(x, axis=-1, keepdims=True)
    xc = x - mu
    var = jnp.mean(xc * xc, axis=-1, keepdims=True)
    return xc * lax.rsqrt(var + LN_EPS) * g + b


def _mm_bias_kernel(x_ref, w_ref, b_ref, s_ref, o_ref):
    acc = jnp.dot(x_ref[...].astype(BF16), w_ref[...], preferred_element_type=F32)
    o_ref[...] = ((acc + b_ref[...]) * s_ref[...]).astype(o_ref.dtype)


def mm_bias(x, w, b, colscale, out_dtype, tm=512):
    t, k = x.shape
    n = w.shape[1]
    tm = min(tm, t)
    return pl.pallas_call(
        _mm_bias_kernel,
        grid=(t // tm,),
        in_specs=[pl.BlockSpec((tm, k), lambda i: (i, 0)),
                  pl.BlockSpec((k, n), lambda i: (0, 0)),
                  pl.BlockSpec((1, n), lambda i: (0, 0)),
                  pl.BlockSpec((1, n), lambda i: (0, 0))],
        out_specs=pl.BlockSpec((tm, n), lambda i: (i, 0)),
        out_shape=jax.ShapeDtypeStruct((t, n), out_dtype),
        compiler_params=_params("parallel"),
        name="mm_bias",
    )(x, w, b, colscale)


def _outproj_ln_kernel(a_ref, w_ref, b_ref, x_ref, g_ref, beta_ref, o_ref):
    mix = jnp.dot(a_ref[...].astype(BF16), w_ref[...], preferred_element_type=F32) + b_ref[...]
    o_ref[...] = _layer_norm(DEEPNORM_ALPHA * x_ref[...] + mix, g_ref[...], beta_ref[...])


def outproj_ln(a, w, b, x, g, beta, tm=512):
    t, d = x.shape
    tm = min(tm, t)
    row = lambda i: (i, 0)
    fix = lambda i: (0, 0)
    return pl.pallas_call(
        _outproj_ln_kernel,
        grid=(t // tm,),
        in_specs=[pl.BlockSpec((tm, d), row), pl.BlockSpec((d, d), fix), pl.BlockSpec((1, d), fix),
                  pl.BlockSpec((tm, d), row), pl.BlockSpec((1, d), fix), pl.BlockSpec((1, d), fix)],
        out_specs=pl.BlockSpec((tm, d), row),
        out_shape=jax.ShapeDtypeStruct((t, d), F32),
        compiler_params=_params("parallel"),
        name="outproj_ln",
    )(a, w, b, x, g, beta)


def _ln_ple_kernel(x_ref, f_ref, p_ref, wg_ref, wp_ref, g_ref, beta_ref, o_ref):
    x2 = _layer_norm(DEEPNORM_ALPHA * x_ref[...] + f_ref[...], g_ref[...], beta_ref[...])
    gate = jax.nn.sigmoid(jnp.dot(x2.astype(BF16), wg_ref[...], preferred_element_type=F32))
    emb = jnp.dot(p_ref[...].astype(BF16), wp_ref[...], preferred_element_type=F32)
    o_ref[...] = x2 + gate * emb


def ln_ple(x, ffn, p, wg, wp, g, beta, tm=512):
    t, d = x.shape
    tm = min(tm, t)
    row = lambda i: (i, 0)
    fix = lambda i: (0, 0)
    return pl.pallas_call(
        _ln_ple_kernel,
        grid=(t // tm,),
        in_specs=[pl.BlockSpec((tm, d), row), pl.BlockSpec((tm, d), row), pl.BlockSpec((tm, PLE_DIM), row),
                  pl.BlockSpec((d, d), fix), pl.BlockSpec((PLE_DIM, d), fix),
                  pl.BlockSpec((1, d), fix), pl.BlockSpec((1, d), fix)],
        out_specs=pl.BlockSpec((tm, d), row),
        out_shape=jax.ShapeDtypeStruct((t, d), F32),
        compiler_params=_params("parallel"),
        name="ln_ple",
    )(x, ffn, p, wg, wp, g, beta)


def _hy_inproj_kernel(x_ref, xp_ref, xn_ref, w_ref, b_ref, wc_ref, bc_ref, o_ref, *, tm, seq):
    i = pl.program_id(1)
    w = w_ref[...]
    b = b_ref[...]
    u = jnp.dot(x_ref[...].astype(BF16), w, preferred_element_type=F32) + b
    up = jnp.dot(xp_ref[...].astype(BF16), w, preferred_element_type=F32) + b
    un = jnp.dot(xn_ref[...].astype(BF16), w, preferred_element_type=F32) + b
    first = (i * tm) % seq == 0
    last = ((i + 1) * tm) % seq == 0
    prev_row = jnp.where(first, 0.0, up[SUBLANES - 1:SUBLANES, :])
    next_row = jnp.where(last, 0.0, un[0:1, :])
    rows = lax.broadcasted_iota(jnp.int32, u.shape, 0)
    um1 = jnp.where(rows == 0, prev_row, pltpu.roll(u, 1, 0))
    up1 = jnp.where(rows == tm - 1, next_row, pltpu.roll(u, tm - 1, 0))
    wc = wc_ref[...]
    o_ref[...] = um1 * wc[0:1, :] + u * wc[1:2, :] + up1 * wc[2:3, :] + bc_ref[...]


def hy_inproj(x, w, b, wc, bc, seq, tm=512):
    t, d = x.shape
    tm = min(tm, seq)
    nblk8 = t // SUBLANES
    per = tm // SUBLANES
    kern = functools.partial(_hy_inproj_kernel, tm=tm, seq=seq)
    return pl.pallas_call(
        kern,
        grid=(3, t // tm),
        in_specs=[pl.BlockSpec((tm, d), lambda j, i: (i, 0)),
                  pl.BlockSpec((SUBLANES, d), lambda j, i: (jnp.maximum(i * per - 1, 0), 0)),
                  pl.BlockSpec((SUBLANES, d), lambda j, i: (jnp.minimum((i + 1) * per, nblk8 - 1), 0)),
                  pl.BlockSpec((d, d), lambda j, i: (0, j)),
                  pl.BlockSpec((1, d), lambda j, i: (0, j)),
                  pl.BlockSpec((3, d), lambda j, i: (0, j)),
                  pl.BlockSpec((1, d), lambda j, i: (0, j))],
        out_specs=pl.BlockSpec((None, tm, d), lambda j, i: (j, i, 0)),
        out_shape=jax.ShapeDtypeStruct((3, t, d), F32),
        compiler_params=_params("parallel", "parallel"),
        name="hy_inproj",
    )(x, x, x, w, b, wc, bc)


def _hy_filter_kernel(feat_ref, w1_ref, b1_ref, w2_ref, b2_ref, w3_ref, dec_ref, o_ref, *, tl):
    hi = lax.Precision.HIGHEST
    feat = feat_ref[...]
    h = jnp.sin(jnp.dot(feat, w1_ref[...], preferred_element_type=F32, precision=hi) + b1_ref[...])
    h = jnp.sin(jnp.dot(h, w2_ref[...], preferred_element_type=F32, precision=hi) + b2_ref[...])
    h = jnp.dot(h, w3_ref[...], preferred_element_type=F32, precision=hi)
    h = h * jnp.exp(-feat[:, 0:1] * jnp.abs(dec_ref[...]))
    pos = pl.program_id(0) * tl + lax.broadcasted_iota(jnp.int32, (tl, D_MODEL), 0)
    for f in range(HY_FILTERS):
        piece = h[:, f * D_MODEL:(f + 1) * D_MODEL]
        if f >= HY_FILTERS // 2:
            piece = jnp.where(pos == 0, 0.0, piece)
        o_ref[f] = piece


def hy_filter(seq, w1, b1, w2, b2, w3, decay, tl=256):
    tl = min(tl, seq)
    t_norm = jnp.linspace(0.0, 1.0, seq, dtype=F32)[:, None]
    t_idx = jnp.arange(seq, dtype=F32)[:, None]
    bands = jnp.linspace(1e-4, HY_BANDS - 1, HY_BANDS, dtype=F32)[None, :]
    phase = bands * t_idx * (2.0 * math.pi / seq)
    feat = jnp.concatenate([t_norm, jnp.cos(phase), -jnp.sin(phase)], axis=-1)
    feat = jnp.pad(feat, ((0, 0), (0, LANES - HY_EMB)))
    padh = LANES - HY_HIDDEN
    w1p = jnp.pad(w1, ((0, LANES - HY_EMB), (0, padh)))
    b1p = jnp.pad(b1, (0, padh))[None, :]
    w2p = jnp.pad(w2, ((0, padh), (0, padh)))
    b2p = jnp.pad(b2, (0, padh))[None, :]
    w3p = jnp.pad(w3, ((0, padh), (0, 0)))
    nf = HY_FILTERS * D_MODEL
    fix = lambda i: (0, 0)
    return pl.pallas_call(
        functools.partial(_hy_filter_kernel, tl=tl),
        grid=(seq // tl,),
        in_specs=[pl.BlockSpec((tl, LANES), lambda i: (i, 0)),
                  pl.BlockSpec((LANES, LANES), fix), pl.BlockSpec((1, LANES), fix),
                  pl.BlockSpec((LANES, LANES), fix), pl.BlockSpec((1, LANES), fix),
                  pl.BlockSpec((LANES, nf), fix), pl.BlockSpec((1, nf), fix)],
        out_specs=pl.BlockSpec((HY_FILTERS, tl, D_MODEL), lambda i: (0, i, 0)),
        out_shape=jax.ShapeDtypeStruct((HY_FILTERS, seq, D_MODEL), F32),
        compiler_params=_params("parallel"),
        name="hy_filter",
    )(feat, w1p, b1p, w2p, b2p, w3p, decay.reshape(1, nf))


def _dft_tables(seq):
    n = 2 * seq
    n2 = FFT_N2
    n1 = n // n2
    n1h = n1 // 2
    two_pi = 2.0 * math.pi

    def ang(m, period):
        return (m % period).astype(F32) * (two_pi / period)

    i_n2 = jnp.arange(n2, dtype=jnp.int32)[:, None, None]
    i_k1 = jnp.arange(n1, dtype=jnp.int32)[None, :, None]
    i_n1 = jnp.arange(n1h, dtype=jnp.int32)[None, None, :]
    th = ang(i_k1 * (n2 * i_n1 + i_n2), n)
    fa = jnp.concatenate([jnp.cos(th), -jnp.sin(th)], axis=1)
    a = jnp.arange(n2, dtype=jnp.int32)
    ph = ang(a[:, None] * a[None, :], n2)
    c, s = jnp.cos(ph), jnp.sin(ph)
    mf = jnp.concatenate([jnp.concatenate([c, s], axis=1), jnp.concatenate([-s, c], axis=1)], axis=0)
    j_k1 = jnp.arange(n1, dtype=jnp.int32)[:, None, None]
    j_n2 = jnp.arange(n2, dtype=jnp.int32)[None, :, None]
    j_k2 = jnp.arange(n2, dtype=jnp.int32)[None, None, :]
    phi = ang(j_n2 * (j_k1 + n1 * j_k2), n)
    c, s = jnp.cos(phi), jnp.sin(phi)
    gi = jnp.concatenate([jnp.concatenate([c, -s], axis=2), jnp.concatenate([s, c], axis=2)], axis=1)
    r_n1 = jnp.arange(n1h, dtype=jnp.int32)[:, None]
    r_k1 = jnp.arange(n1, dtype=jnp.int32)[None, :]
    psi = ang(r_n1 * r_k1, n1)
    fb = jnp.concatenate([jnp.cos(psi), -jnp.sin(psi)], axis=1) * (1.0 / n)
    return dict(n1=n1, n2=n2, n1h=n1h, fa=fa, mf=mf, gi=gi, fb=fb)


def _dot(a, b, exact):
    if exact:
        return jnp.dot(a, b, preferred_element_type=F32, precision=lax.Precision.HIGHEST)
    return jnp.dot(a, b.astype(BF16), preferred_element_type=F32)


def _fft_a_kernel(x_ref, fa_ref, o_ref, *, n1, exact):
    for j in range(SUBLANES):
        r = _dot(fa_ref[j], x_ref[:, j, :], exact)
        o_ref[0, :, j, :] = r[:n1]
        o_ref[1, :, j, :] = r[n1:]


def fft_stage_a(x, tab, exact=False):
    bz, seq, d = x.shape
    n1, n2, n1h = tab["n1"], tab["n2"], tab["n1h"]
    g8 = n2 // SUBLANES
    fa = tab["fa"] if exact else tab["fa"].astype(BF16)
    x5 = x.reshape(bz, n1h, g8, SUBLANES, d)
    out = pl.pallas_call(
        functools.partial(_fft_a_kernel, n1=n1, exact=exact),
        grid=(bz, g8),
        in_specs=[pl.BlockSpec((None, n1h, None, SUBLANES, d), lambda b, g: (b, 0, g, 0, 0)),
                  pl.BlockSpec((SUBLANES, 2 * n1, n1h), lambda b, g: (g, 0, 0))],
        out_specs=pl.BlockSpec((None, 2, n1, None, SUBLANES, d), lambda b, g: (b, 0, 0, g, 0, 0)),
        out_shape=jax.ShapeDtypeStruct((bz, 2, n1, g8, SUBLANES, d), F32),
        compiler_params=_params("parallel", "parallel"),
        name="fft_stage_a",
    )(x5, fa)
    return out.reshape(bz, 2, n1, n2, d)


def _filter_spec_kernel(cf_ref, cb_ref, mf_ref, o_ref, *, n2):
    mf = mf_ref[...]
    yf = _dot(mf, jnp.concatenate([cf_ref[0], cf_ref[1]], axis=0), True)
    yb = _dot(mf, jnp.concatenate([cb_ref[0], cb_ref[1]], axis=0), True)
    o_ref[0] = yf[:n2] + yb[:n2]
    o_ref[1] = yf[n2:] - yb[n2:]


def filter_spectrum(cf, tab):
    n1, n2 = tab["n1"], tab["n2"]
    d = cf.shape[-1]
    norder = HY_FILTERS // 2
    blk = (None, 2, None, n2, d)
    return pl.pallas_call(
        functools.partial(_filter_spec_kernel, n2=n2),
        grid=(norder, n1),
        in_specs=[pl.BlockSpec(blk, lambda o, k: (o, 0, k, 0, 0)),
                  pl.BlockSpec(blk, lambda o, k: (norder + o, 0, k, 0, 0)),
                  pl.BlockSpec((2 * n2, 2 * n2), lambda o, k: (0, 0))],
        out_specs=pl.BlockSpec(blk, lambda o, k: (o, 0, k, 0, 0)),
        out_shape=jax.ShapeDtypeStruct((norder, 2, n1, n2, d), F32),
        compiler_params=_params("parallel", "parallel"),
        name="filter_spectrum",
    )(cf, cf, tab["mf"])


def _fft_mid_kernel(c_ref, kf_ref, mf_ref, gi_ref, o_ref, *, n2):
    c = jnp.concatenate([c_ref[0], c_ref[1]], axis=0)
    y = _dot(mf_ref[...], c, False)
    yr, yi = y[:n2], y[n2:]
    kr, ki = kf_ref[0], kf_ref[1]
    z = jnp.concatenate([yr * kr - yi * ki, yr * ki + yi * kr], axis=0)
    dd = _dot(gi_ref[...], z, False)
    o_ref[0] = dd[:n2]
    o_ref[1] = dd[n2:]


def fft_mid(c, kf, tab):
    bz, _, n1, n2, d = c.shape
    blk = (None, 2, None, n2, d)
    return pl.pallas_call(
        functools.partial(_fft_mid_kernel, n2=n2),
        grid=(n1, bz),
        in_specs=[pl.BlockSpec(blk, lambda k, b: (b, 0, k, 0, 0)),
                  pl.BlockSpec((2, None, n2, d), lambda k, b: (0, k, 0, 0)),
                  pl.BlockSpec((2 * n2, 2 * n2), lambda k, b: (0, 0)),
                  pl.BlockSpec((None, 2 * n2, 2 * n2), lambda k, b: (k, 0, 0))],
        out_specs=pl.BlockSpec(blk, lambda k, b: (b, 0, k, 0, 0)),
        out_shape=jax.ShapeDtypeStruct((bz, 2, n1, n2, d), F32),
        compiler_params=_params("parallel", "parallel"),
        name="fft_mid",
    )(c, kf, tab["mf"].astype(BF16), tab["gi"].astype(BF16))


def _fft_b_kernel(d_ref, fb_ref, z_ref, g_ref, skip_ref, o_ref):
    fb = fb_ref[...]
    skip = skip_ref[...]
    for j in range(SUBLANES):
        dd = jnp.concatenate([d_ref[0, :, j, :], d_ref[1, :, j, :]], axis=0)
        y = _dot(fb, dd, False)
        o_ref[:, j, :] = (y + skip * z_ref[:, j, :]) * g_ref[:, j, :]


def fft_stage_b(dp, tab, z, gate, skip):
    bz, _, n1, n2, d = dp.shape
    n1h = tab["n1h"]
    g8 = n2 // SUBLANES
    seq = z.shape[1]
    dp6 = dp.reshape(bz, 2, n1, g8, SUBLANES, d)
    view = lambda a: a.reshape(bz, n1h, g8, SUBLANES, d)
    tok = pl.BlockSpec((None, n1h, None, SUBLANES, d), lambda b, g: (b, 0, g, 0, 0))
    out = pl.pallas_call(
        _fft_b_kernel,
        grid=(bz, g8),
        in_specs=[pl.BlockSpec((None, 2, n1, None, SUBLANES, d), lambda b, g: (b, 0, 0, g, 0, 0)),
                  pl.BlockSpec((n1h, 2 * n1), lambda b, g: (0, 0)),
                  tok, tok,
                  pl.BlockSpec((1, d), lambda b, g: (0, 0))],
        out_specs=tok,
        out_shape=jax.ShapeDtypeStruct((bz, n1h, g8, SUBLANES, d), F32),
        compiler_params=_params("parallel", "parallel"),
        name="fft_stage_b",
    )(dp6, tab["fb"].astype(BF16), view(z), view(gate), skip)
    return out.reshape(bz, seq, d)


def hyena_mixer(xf, bsz, seq, w, j):
    d = D_MODEL
    uvg = hy_inproj(xf, w["hy_w_in"][j].astype(BF16), w["hy_b_in"][j][None, :],
                    w["hy_w_conv"][j], w["hy_b_conv"][j][None, :], seq)
    uvg = uvg.reshape(3, bsz, seq, d)
    tab = _dft_tables(seq)
    hk = hy_filter(seq, w["hy_f_w1"][j], w["hy_f_b1"][j], w["hy_f_w2"][j], w["hy_f_b2"][j],
                   w["hy_f_w3"][j], w["hy_decay"][j])
    kf = filter_spectrum(fft_stage_a(hk, tab, exact=True), tab)
    z = uvg[0]
    for order in range(2):
        c = fft_stage_a(z, tab)
        dp = fft_mid(c, kf[order], tab)
        z = fft_stage_b(dp, tab, z, uvg[1 + order], w["hy_skip"][j][order][None, :])
    return z.reshape(bsz * seq, d)


def _na_bias_kernel(rpb_ref, o_ref):
    h = pl.program_id(0)
    q = lax.broadcasted_iota(jnp.int32, (GRID_W, GRID_W), 0)
    kc = lax.broadcasted_iota(jnp.int32, (GRID_W, GRID_W), 1)
    win_start = jnp.clip(q - WIN_C // 2, 0, GRID_W - WIN_C)
    valid = (kc >= win_start) & (kc < win_start + WIN_C)
    dc = jnp.clip(kc - q + WIN_C - 1, 0, 2 * WIN_C - 2)
    nrow = 2 * WIN_R - 1
    for dr in range(nrow):
        t = jnp.zeros((GRID_W, GRID_W), F32)
        for c in range(2 * WIN_C - 1):
            t = jnp.where(dc == c, rpb_ref[h * nrow + dr, c], t)
        o_ref[dr] = jnp.where(valid, t, NEG_INF)


def na_bias_table(rpb):
    nrow = 2 * WIN_R - 1
    ncol = 2 * WIN_C - 1
    return pl.pallas_call(
        _na_bias_kernel,
        grid=(NA_HEADS,),
        in_specs=[pl.BlockSpec(memory_space=pltpu.SMEM)],
        out_specs=pl.BlockSpec((None, nrow, GRID_W, GRID_W), lambda h: (h, 0, 0, 0)),
        out_shape=jax.ShapeDtypeStruct((NA_HEADS, nrow, GRID_W, GRID_W), F32),
        compiler_params=_params("parallel"),
        name="na_bias",
    )(rpb.reshape(NA_HEADS * nrow, ncol))


def _na_attn_kernel(q_ref, k_ref, v_ref, *rest, rows):
    bias_refs, o_ref = rest[:-1], rest[-1]
    step = pl.program_id(2)
    gl = NA_HEAD_GROUP * NA_HEAD_DIM
    nk = WIN_R * GRID_W
    stacked = NA_HEAD_GROUP * GRID_W
    row_head = lax.broadcasted_iota(jnp.int32, (stacked, gl), 0) // GRID_W
    lane_head = lax.broadcasted_iota(jnp.int32, (stacked, gl), 1) // NA_HEAD_DIM
    own = row_head == lane_head
    out_head = lax.broadcasted_iota(jnp.int32, (GRID_W, gl), 1) // NA_HEAD_DIM
    for t, bias_ref in enumerate(bias_refs):
        r = step * NA_ROWS + t
        rs = jnp.clip(r - WIN_R // 2, 0, rows - WIN_R)
        start = pl.multiple_of(rs * GRID_W, GRID_W)
        kb = k_ref[pl.ds(start, nk), :]
        vb = v_ref[pl.ds(start, nk), :]
        q = q_ref[t * GRID_W:(t + 1) * GRID_W, :]
        qs = jnp.where(own, jnp.concatenate([q] * NA_HEAD_GROUP, axis=0), jnp.zeros((), q.dtype))
        s = lax.dot_general(qs, kb, (((1,), (1,)), ((), ())), preferred_element_type=F32)
        s = s + bias_ref[...].reshape(stacked, nk)
        m = jnp.max(s, axis=-1, keepdims=True)
        e = jnp.exp(s - m)
        p = e * (1.0 / jnp.sum(e, axis=-1, keepdims=True))
        oall = jnp.dot(p.astype(BF16), vb, preferred_element_type=F32)
        o = jnp.zeros((GRID_W, gl), F32)
        for hh in range(NA_HEAD_GROUP):
            o = jnp.where(out_head == hh, oall[hh * GRID_W:(hh + 1) * GRID_W, :], o)
        o_ref[t * GRID_W:(t + 1) * GRID_W, :] = o


def na_attention(qkv, bias, bsz, seq):
    rows = seq // GRID_W
    t = bsz * seq
    gl = NA_HEAD_GROUP * NA_HEAD_DIM
    ngrp = NA_HEADS // NA_HEAD_GROUP

    def offset(r):
        return r - jnp.clip(r - WIN_R // 2, 0, rows - WIN_R)

    nstep = rows // NA_ROWS
    tok = pl.BlockSpec((NA_ROWS * GRID_W, gl), lambda b, g, r: (b * nstep + r, g))

    def bias_spec(t):
        return pl.BlockSpec((NA_HEAD_GROUP, None, GRID_W, WIN_R * GRID_W),
                            lambda b, g, r: (g, offset(r * NA_ROWS + t), 0, 0))

    return pl.pallas_call(
        functools.partial(_na_attn_kernel, rows=rows),
        grid=(bsz, ngrp, nstep),
        in_specs=[tok,
                  pl.BlockSpec((seq, gl), lambda b, g, r: (b, ngrp + g)),
                  pl.BlockSpec((seq, gl), lambda b, g, r: (b, 2 * ngrp + g))]
                 + [bias_spec(t) for t in range(NA_ROWS)],
        out_specs=tok,
        out_shape=jax.ShapeDtypeStruct((t, D_MODEL), F32),
        compiler_params=_params("parallel", "parallel", "arbitrary"),
        name="na_attention",
    )(qkv, qkv, qkv, *([bias] * NA_ROWS))


def na_mixer(xf, bsz, seq, w, j):
    d = D_MODEL
    colscale = jnp.concatenate([jnp.full((1, d), NA_HEAD_DIM ** -0.5, F32), jnp.ones((1, 2 * d), F32)], axis=1)
    qkv = mm_bias(xf, w["na_w_qkv"][j].astype(BF16), w["na_b_qkv"][j][None, :], colscale, BF16)
    tb = na_bias_table(w["na_rpb"][j])
    dr = jnp.arange(WIN_R)[None, :] - jnp.arange(WIN_R)[:, None] + WIN_R - 1
    bias = tb[:, dr]
    bias = bias.transpose(0, 1, 3, 2, 4).reshape(NA_HEADS, WIN_R, GRID_W, WIN_R * GRID_W)
    return na_attention(qkv, bias, bsz, seq)


def _pair_candidates(v1, i1, v2, i2):
    tt = v1.shape[1]
    sub = lax.broadcasted_iota(jnp.int32, (SUBLANES, tt), 0)
    vals, keys, ids = [], [], []
    for a in range(PEER_TOPK // 2):
        nb = PEER_TOPK // (a + 1)
        va = v1[a:a + 1, :]
        ia = i1[a:a + 1, :] * PEER_KEYS
        for b0 in range(0, nb, SUBLANES):
            val = va + v2[b0:b0 + SUBLANES, :]
            if nb - b0 < SUBLANES:
                val = jnp.where(sub < nb - b0, val, -jnp.inf)
            vals.append(val)
            keys.append(a * PEER_TOPK + b0 + sub)
            ids.append(ia + i2[b0:b0 + SUBLANES, :])
    half = PEER_TOPK // 2
    vals.append(v1[half:, :] + v2[0:1, :])
    keys.append((half + sub) * PEER_TOPK)
    ids.append(i1[half:, :] * PEER_KEYS + i2[0:1, :])
    cat = lambda xs: jnp.concatenate(xs, axis=0)
    return cat(vals), cat(keys), cat(ids)


PEER_GROUP = SUBLANES
PEER_SETS = 4
PEER_AHEAD = 2
PEER_TILE = 128
PEER_RING = 3
PEER_SPLIT = PEER_TOPK // 2


def _round(s, key):
    m = jnp.max(s, axis=0, keepdims=True)
    am = jnp.min(jnp.where(s == m, key, jnp.iinfo(jnp.int32).max), axis=0, keepdims=True)
    hit = key == am
    return m, am, hit, jnp.where(hit, -jnp.inf, s)


def _peer_fused_kernel(x_ref, xn1_ref, xn2_ref, wq_ref, k1_ref, k2_ref, tab_ref, o_ref,
                       buf, sem, idx_s, gate_s, stage_i, stage_g, s2_s, v1_s, i1_s, v2_s, i2_s, tr_i, dsem,
                       *, ntile):
    d = D_MODEL
    tt = PEER_TILE
    ngroup = tt // PEER_GROUP
    half_rows = PEER_PAIRS // 2
    tile = pl.program_id(0)
    cat = lambda xs: jnp.concatenate(xs, axis=0)
    nt_dims = (((1,), (1,)), ((), ()))

    def ring(k):
        return lax.rem(tile + k, PEER_RING)

    def route_a(h, xr_ref):
        q = jnp.dot(xr_ref[...].astype(BF16), wq_ref[h], preferred_element_type=F32).astype(BF16)
        s1 = lax.dot_general(k1_ref[...], q[:, :PEER_HALF], nt_dims, preferred_element_type=F32)
        s2 = lax.dot_general(k2_ref[...], q[:, PEER_HALF:], nt_dims, preferred_element_type=F32)
        rowid = lax.broadcasted_iota(jnp.int32, s1.shape, 0)
        yield
        vals, ids = [], []
        for _ in range(PEER_TOPK):
            m, am, _, s1 = _round(s1, rowid)
            vals.append(m)
            ids.append(am)
            yield
        vals2, ids2 = [], []
        for _ in range(PEER_SPLIT):
            m, am, _, s2 = _round(s2, rowid)
            vals2.append(m)
            ids2.append(am)
            yield
        v1_s[...] = cat(vals)
        i1_s[...] = cat(ids)
        v2_s[0:PEER_SPLIT, :] = cat(vals2)
        i2_s[0:PEER_SPLIT, :] = cat(ids2)
        s2_s[...] = s2

    def route_b(h):
        s2 = s2_s[...]
        v1 = v1_s[...]
        i1 = i1_s[...]
        v2lo = v2_s[0:PEER_SPLIT, :]
        i2lo = i2_s[0:PEER_SPLIT, :]
        rowid = lax.broadcasted_iota(jnp.int32, s2.shape, 0)
        yield
        vals2, ids2 = [], []
        for _ in range(PEER_TOPK - PEER_SPLIT):
            m, am, _, s2 = _round(s2, rowid)
            vals2.append(m)
            ids2.append(am)
            yield
        v2 = cat([v2lo] + vals2)
        i2 = cat([i2lo] + ids2)
        cand, ckey, cidx = _pair_candidates(v1, i1, v2, i2)
        yield
        tops, tids = [], []
        for _ in range(PEER_TOPK):
            m, _, hit, cand = _round(cand, ckey)
            tops.append(m)
            tids.append(jnp.max(jnp.where(hit, cidx, -1), axis=0, keepdims=True))
            yield
        top = cat(tops)
        e = jnp.exp(top - top[0:1, :])
        row0 = h * PEER_TOPK
        rows = pl.ds(row0 if isinstance(row0, int) else pl.multiple_of(row0, PEER_TOPK), PEER_TOPK)
        stage_g[rows, :] = e / jnp.sum(e, axis=0, keepdims=True)
        stage_i[rows, :] = cat(tids)

    def publish(slot):
        gate_s[slot] = stage_g[...].T
        tr_i[...] = stage_i[...].T
        cp = pltpu.make_async_copy(tr_i, idx_s.at[slot], dsem.at[0])
        cp.start()
        cp.wait()

    def issue(islot, tok, slot, r0, r1):
        for r in range(r0, r1):
            e = idx_s[islot, tok, r]
            cp = pltpu.make_async_copy(tab_ref.at[e], buf.at[slot, pl.ds(r, 1), :], sem.at[slot])
            cp.start(priority=r % 2)

    def wait(slot):
        pltpu.make_async_copy(tab_ref.at[pl.ds(0, PEER_PAIRS), 0, :], buf.at[slot], sem.at[slot]).wait()

    def step(g, gset, islot, ptok0, routing):
        base = gset * PEER_GROUP
        nxt = ((gset + PEER_AHEAD) % PEER_SETS) * PEER_GROUP
        cur = ring(0)
        for j in range(PEER_GROUP):
            wait(base + j)
        next(routing, None)
        tok0 = pl.multiple_of(g * PEER_GROUP, PEER_GROUP)
        x8 = x_ref[pl.ds(tok0, PEER_GROUP), :].astype(BF16)
        rowid = lax.broadcasted_iota(jnp.int32, (PEER_GROUP, PEER_PAIRS), 0)
        act = jnp.zeros((PEER_GROUP, PEER_PAIRS), F32)
        for j in range(PEER_GROUP):
            issue(islot, ptok0 + j, nxt + j, 0, half_rows)
            u = buf[base + j, :, 0:d].astype(BF16)
            a = lax.dot_general(x8, u, nt_dims, preferred_element_type=F32)
            act = jnp.where(rowid == j, a, act)
            next(routing, None)
            next(routing, None)
        gelu = 0.5 * act * (1.0 + lax.erf(act * (2.0 ** -0.5)))
        cb = (gate_s[cur, pl.ds(tok0, PEER_GROUP), :] * gelu).astype(BF16)
        orow = lax.broadcasted_iota(jnp.int32, (PEER_GROUP, d), 0)
        out = jnp.zeros((PEER_GROUP, d), F32)
        for j in range(PEER_GROUP):
            issue(islot, ptok0 + j, nxt + j, half_rows, PEER_PAIRS)
            v = buf[base + j, :, d:2 * d].astype(BF16)
            o = jnp.dot(cb, v, preferred_element_type=F32)
            out = jnp.where(orow == j, o, out)
            next(routing, None)
            next(routing, None)
        for _ in routing:
            pass
        o_ref[pl.ds(tok0, PEER_GROUP), :] = out

    def routing_for(g, k):
        h = g // 2 if isinstance(g, int) else lax.shift_right_logical(g, 1)
        return route_a(h, xn2_ref) if k % 2 == 0 else route_b(h)

    @pl.when(tile == 0)
    def _():
        for slot, xr_ref in ((0, x_ref), (1, xn1_ref)):
            def head(h, carry, xr_ref=xr_ref):
                for _ in route_a(h, xr_ref):
                    pass
                for _ in route_b(h):
                    pass
                return carry

            lax.fori_loop(0, PEER_HEADS, head, 0)
            publish(slot)
        for g in range(PEER_AHEAD):
            for j in range(PEER_GROUP):
                issue(0, g * PEER_GROUP + j, g * PEER_GROUP + j, 0, PEER_PAIRS)

    def body(gg, carry):
        for k in range(PEER_SETS):
            g = gg * PEER_SETS + k
            step(g, k, ring(0), (g + PEER_AHEAD) * PEER_GROUP, routing_for(g, k))
        return carry

    lax.fori_loop(0, ngroup // PEER_SETS - 1, body, 0)
    for k in range(PEER_SETS):
        g = ngroup - PEER_SETS + k
        ahead = g + PEER_AHEAD
        if ahead < ngroup:
            step(g, k, ring(0), ahead * PEER_GROUP, routing_for(g, k))
        else:
            step(g, k, ring(1), (ahead - ngroup) * PEER_GROUP, routing_for(g, k))
    publish(ring(2))

    @pl.when(tile == ntile - 1)
    def _():
        for k in range(PEER_AHEAD):
            for j in range(PEER_GROUP):
                wait(k * PEER_GROUP + j)


def peer_fused(x, wq, k1, k2, tab):
    t, d = x.shape
    tt = PEER_TILE
    ntile = t // tt
    ngroup = tt // PEER_GROUP
    assert t % tt == 0 and ntile >= PEER_RING
    assert ngroup % PEER_SETS == 0 and ngroup >= 2 * PEER_SETS and ngroup == 2 * PEER_HEADS
    ahead = lambda k: (lambda i: (jnp.minimum(i + k, ntile - 1), 0))
    fix2 = lambda i: (0, 0)
    f32s = pltpu.VMEM((PEER_TOPK, tt), F32)
    i32s = pltpu.VMEM((PEER_TOPK, tt), jnp.int32)
    return pl.pallas_call(
        functools.partial(_peer_fused_kernel, ntile=ntile),
        grid=(ntile,),
        in_specs=[pl.BlockSpec((tt, d), ahead(0)), pl.BlockSpec((tt, d), ahead(1)), pl.BlockSpec((tt, d), ahead(2)),
                  pl.BlockSpec((PEER_HEADS, d, PEER_QDIM), lambda i: (0, 0, 0)),
                  pl.BlockSpec((PEER_KEYS, PEER_HALF), fix2), pl.BlockSpec((PEER_KEYS, PEER_HALF), fix2),
                  pl.BlockSpec(memory_space=pl.ANY)],
        out_specs=pl.BlockSpec((tt, d), ahead(0)),
        out_shape=jax.ShapeDtypeStruct((t, d), F32),
        scratch_shapes=[pltpu.VMEM((PEER_SETS * PEER_GROUP, PEER_PAIRS, 2 * d), F32),
                        pltpu.SemaphoreType.DMA((PEER_SETS * PEER_GROUP,)),
                        pltpu.SMEM((PEER_RING, tt, PEER_PAIRS), jnp.int32),
                        pltpu.VMEM((PEER_RING, tt, PEER_PAIRS), F32),
                        pltpu.VMEM((PEER_PAIRS, tt), jnp.int32),
                        pltpu.VMEM((PEER_PAIRS, tt), F32),
                        pltpu.VMEM((PEER_KEYS, tt), F32),
                        f32s, i32s, f32s, i32s,
                        pltpu.VMEM((tt, PEER_PAIRS), jnp.int32),
                        pltpu.SemaphoreType.DMA((1,))],
        compiler_params=_params("arbitrary"),
        name="peer_fused",
    )(x, x, x, wq, k1, k2, tab)


def peer_ffn(xf, w, i):
    wq = w["peer_w_q"][i].astype(BF16).reshape(D_MODEL, PEER_HEADS, PEER_QDIM).transpose(1, 0, 2)
    tab = jnp.concatenate([w["peer_u"][i], w["peer_v"][i]], axis=1)[:, None, :]
    return peer_fused(xf, wq, w["peer_k1"][i].astype(BF16), w["peer_k2"][i].astype(BF16), tab)


def trunk(x, p, w):
    bsz, seq, d = x.shape
    t = bsz * seq
    xf = x.reshape(t, d)
    pf = p.reshape(DEPTH, t, PLE_DIM)
    for i in range(DEPTH):
        j = i // 2
        if i % 2 == 0:
            mix = hyena_mixer(xf, bsz, seq, w, j)
            w_out, b_out = w["hy_w_out"][j], w["hy_b_out"][j]
        else:
            mix = na_mixer(xf, bsz, seq, w, j)
            w_out, b_out = w["na_w_out"][j], w["na_b_out"][j]
        xf = outproj_ln(mix, w_out.astype(BF16), b_out[None, :], xf, w["ln1_g"][i][None, :], w["ln1_b"][i][None, :])
        ffn = peer_ffn(xf, w, i)
        xf = ln_ple(xf, ffn, pf[i], w["ple_gate_w"][i].astype(BF16), w["ple_w"][i].astype(BF16),
                    w["ln2_g"][i][None, :], w["ln2_b"][i][None, :])
    return xf.reshape(bsz, seq, d)


def kernel(x_prompt, x_sample, p_prompt, p_sample, hy_w_in, hy_b_in, hy_w_conv, hy_b_conv, hy_f_w1, hy_f_b1,
           hy_f_w2, hy_f_b2, hy_f_w3, hy_decay, hy_skip, hy_w_out, hy_b_out, na_w_qkv, na_b_qkv, na_rpb,
           na_w_out, na_b_out, ln1_g, ln1_b, ln2_g, ln2_b, peer_w_q, peer_k1, peer_k2, peer_u, peer_v,
           ple_w, ple_gate_w):
    w = dict(hy_w_in=hy_w_in, hy_b_in=hy_b_in, hy_w_conv=hy_w_conv, hy_b_conv=hy_b_conv,
             hy_f_w1=hy_f_w1, hy_f_b1=hy_f_b1, hy_f_w2=hy_f_w2, hy_f_b2=hy_f_b2, hy_f_w3=hy_f_w3,
             hy_decay=hy_decay, hy_skip=hy_skip, hy_w_out=hy_w_out, hy_b_out=hy_b_out,
             na_w_qkv=na_w_qkv, na_b_qkv=na_b_qkv, na_rpb=na_rpb, na_w_out=na_w_out, na_b_out=na_b_out,
             ln1_g=ln1_g, ln1_b=ln1_b, ln2_g=ln2_g, ln2_b=ln2_b,
             peer_w_q=peer_w_q, peer_k1=peer_k1, peer_k2=peer_k2, peer_u=peer_u, peer_v=peer_v,
             ple_w=ple_w, ple_gate_w=ple_gate_w)
    return (trunk(x_prompt, p_prompt, w), trunk(x_sample, p_sample, w))
```

```python
import functools
import math

import jax
import jax.numpy as jnp
from jax import lax
from jax.experimental import pallas as pl
from jax.experimental.pallas import tpu as pltpu

F32 = jnp.float32
BF16 = jnp.bfloat16

D_MODEL = 1024
DEPTH = 2
DEEPNORM_ALPHA = (2.0 * DEPTH) ** 0.25
LN_EPS = 1e-5
NEG_INF = -1e30
PLE_DIM = 256

GRID_W = 64
HY_BANDS = 16
HY_EMB = 1 + 2 * HY_BANDS
HY_HIDDEN = 64
HY_FILTERS = 4

NA_HEADS = 16
NA_HEAD_DIM = D_MODEL // NA_HEADS
WIN_R = 8
WIN_C = 16
NA_HEAD_GROUP = 4
NA_ROWS = 2

PEER_HEADS = 8
PEER_KEYS = 128
PEER_QDIM = 256
PEER_HALF = PEER_QDIM // 2
PEER_TOPK = 16
PEER_PAIRS = PEER_HEADS * PEER_TOPK

LANES = 128
SUBLANES = 8
FFT_N2 = 128
VMEM_LIMIT = 56 * 1024 * 1024


def _params(*sem):
    return pltpu.CompilerParams(dimension_semantics=sem, vmem_limit_bytes=VMEM_LIMIT)


def _layer_norm(x, g, b):
    mu = jnp.mean(x, axis=-1, keepdims=True)
    xc = x - mu
    var = jnp.mean(xc * xc, axis=-1, keepdims=True)
    return xc * lax.rsqrt(var + LN_EPS) * g + b


def _mm_bias_kernel(x_ref, w_ref, b_ref, s_ref, o_ref):
    acc = jnp.dot(x_ref[...].astype(BF16), w_ref[...], preferred_element_type=F32)
    o_ref[...] = ((acc + b_ref[...]) * s_ref[...]).astype(o_ref.dtype)


def mm_bias(x, w, b, colscale, out_dtype, tm=512):
    t, k = x.shape
    n = w.shape[1]
    tm = min(tm, t)
    return pl.pallas_call(
        _mm_bias_kernel,
        grid=(t // tm,),
        in_specs=[pl.BlockSpec((tm, k), lambda i: (i, 0)),
                  pl.BlockSpec((k, n), lambda i: (0, 0)),
                  pl.BlockSpec((1, n), lambda i: (0, 0)),
                  pl.BlockSpec((1, n), lambda i: (0, 0))],
        out_specs=pl.BlockSpec((tm, n), lambda i: (i, 0)),
        out_shape=jax.ShapeDtypeStruct((t, n), out_dtype),
        compiler_params=_params("parallel"),
        name="mm_bias",
    )(x, w, b, colscale)


def _outproj_ln_kernel(a_ref, w_ref, b_ref, x_ref, g_ref, beta_ref, o_ref):
    mix = jnp.dot(a_ref[...].astype(BF16), w_ref[...], preferred_element_type=F32) + b_ref[...]
    o_ref[...] = _layer_norm(DEEPNORM_ALPHA * x_ref[...] + mix, g_ref[...], beta_ref[...])


def outproj_ln(a, w, b, x, g, beta, tm=512):
    t, d = x.shape
    tm = min(tm, t)
    row = lambda i: (i, 0)
    fix = lambda i: (0, 0)
    return pl.pallas_call(
        _outproj_ln_kernel,
        grid=(t // tm,),
        in_specs=[pl.BlockSpec((tm, d), row), pl.BlockSpec((d, d), fix), pl.BlockSpec((1, d), fix),
                  pl.BlockSpec((tm, d), row), pl.BlockSpec((1, d), fix), pl.BlockSpec((1, d), fix)],
        out_specs=pl.BlockSpec((tm, d), row),
        out_shape=jax.ShapeDtypeStruct((t, d), F32),
        compiler_params=_params("parallel"),
        name="outproj_ln",
    )(a, w, b, x, g, beta)


def _ln_ple_kernel(x_ref, f_ref, p_ref, wg_ref, wp_ref, g_ref, beta_ref, o_ref):
    x2 = _layer_norm(DEEPNORM_ALPHA * x_ref[...] + f_ref[...], g_ref[...], beta_ref[...])
    gate = jax.nn.sigmoid(jnp.dot(x2.astype(BF16), wg_ref[...], preferred_element_type=F32))
    emb = jnp.dot(p_ref[...].astype(BF16), wp_ref[...], preferred_element_type=F32)
    o_ref[...] = x2 + gate * emb


def ln_ple(x, ffn, p, wg, wp, g, beta, tm=512):
    t, d = x.shape
    tm = min(tm, t)
    row = lambda i: (i, 0)
    fix = lambda i: (0, 0)
    return pl.pallas_call(
        _ln_ple_kernel,
        grid=(t // tm,),
        in_specs=[pl.BlockSpec((tm, d), row), pl.BlockSpec((tm, d), row), pl.BlockSpec((tm, PLE_DIM), row),
                  pl.BlockSpec((d, d), fix), pl.BlockSpec((PLE_DIM, d), fix),
                  pl.BlockSpec((1, d), fix), pl.BlockSpec((1, d), fix)],
        out_specs=pl.BlockSpec((tm, d), row),
        out_shape=jax.ShapeDtypeStruct((t, d), F32),
        compiler_params=_params("parallel"),
        name="ln_ple",
    )(x, ffn, p, wg, wp, g, beta)


def _hy_inproj_kernel(x_ref, xp_ref, xn_ref, w_ref, b_ref, wc_ref, bc_ref, o_ref, *, tm, seq):
    i = pl.program_id(1)
    w = w_ref[...]
    b = b_ref[...]
    u = jnp.dot(x_ref[...].astype(BF16), w, preferred_element_type=F32) + b
    up = jnp.dot(xp_ref[...].astype(BF16), w, preferred_element_type=F32) + b
    un = jnp.dot(xn_ref[...].astype(BF16), w, preferred_element_type=F32) + b
    first = (i * tm) % seq == 0
    last = ((i + 1) * tm) % seq == 0
    prev_row = jnp.where(first, 0.0, up[SUBLANES - 1:SUBLANES, :])
    next_row = jnp.where(last, 0.0, un[0:1, :])
    rows = lax.broadcasted_iota(jnp.int32, u.shape, 0)
    um1 = jnp.where(rows == 0, prev_row, pltpu.roll(u, 1, 0))
    up1 = jnp.where(rows == tm - 1, next_row, pltpu.roll(u, tm - 1, 0))
    wc = wc_ref[...]
    o_ref[...] = um1 * wc[0:1, :] + u * wc[1:2, :] + up1 * wc[2:3, :] + bc_ref[...]


def hy_inproj(x, w, b, wc, bc, seq, tm=512):
    t, d = x.shape
    tm = min(tm, seq)
    nblk8 = t // SUBLANES
    per = tm // SUBLANES
    kern = functools.partial(_hy_inproj_kernel, tm=tm, seq=seq)
    return pl.pallas_call(
        kern,
        grid=(3, t // tm),
        in_specs=[pl.BlockSpec((tm, d), lambda j, i: (i, 0)),
                  pl.BlockSpec((SUBLANES, d), lambda j, i: (jnp.maximum(i * per - 1, 0), 0)),
                  pl.BlockSpec((SUBLANES, d), lambda j, i: (jnp.minimum((i + 1) * per, nblk8 - 1), 0)),
                  pl.BlockSpec((d, d), lambda j, i: (0, j)),
                  pl.BlockSpec((1, d), lambda j, i: (0, j)),
                  pl.BlockSpec((3, d), lambda j, i: (0, j)),
                  pl.BlockSpec((1, d), lambda j, i: (0, j))],
        out_specs=pl.BlockSpec((None, tm, d), lambda j, i: (j, i, 0)),
        out_shape=jax.ShapeDtypeStruct((3, t, d), F32),
        compiler_params=_params("parallel", "parallel"),
        name="hy_inproj",
    )(x, x, x, w, b, wc, bc)


def _hy_filter_kernel(feat_ref, w1_ref, b1_ref, w2_ref, b2_ref, w3_ref, dec_ref, o_ref, *, tl):
    hi = lax.Precision.HIGHEST
    feat = feat_ref[...]
    h = jnp.sin(jnp.dot(feat, w1_ref[...], preferred_element_type=F32, precision=hi) + b1_ref[...])
    h = jnp.sin(jnp.dot(h, w2_ref[...], preferred_element_type=F32, precision=hi) + b2_ref[...])
    h = jnp.dot(h, w3_ref[...], preferred_element_type=F32, precision=hi)
    h = h * jnp.exp(-feat[:, 0:1] * jnp.abs(dec_ref[...]))
    pos = pl.program_id(0) * tl + lax.broadcasted_iota(jnp.int32, (tl, D_MODEL), 0)
    for f in range(HY_FILTERS):
        piece = h[:, f * D_MODEL:(f + 1) * D_MODEL]
        if f >= HY_FILTERS // 2:
            piece = jnp.where(pos == 0, 0.0, piece)
        o_ref[f] = piece


def hy_filter(seq, w1, b1, w2, b2, w3, decay, tl=256):
    tl = min(tl, seq)
    t_norm = jnp.linspace(0.0, 1.0, seq, dtype=F32)[:, None]
    t_idx = jnp.arange(seq, dtype=F32)[:, None]
    bands = jnp.linspace(1e-4, HY_BANDS - 1, HY_BANDS, dtype=F32)[None, :]
    phase = bands * t_idx * (2.0 * math.pi / seq)
    feat = jnp.concatenate([t_norm, jnp.cos(phase), -jnp.sin(phase)], axis=-1)
    feat = jnp.pad(feat, ((0, 0), (0, LANES - HY_EMB)))
    padh = LANES - HY_HIDDEN
    w1p = jnp.pad(w1, ((0, LANES - HY_EMB), (0, padh)))
    b1p = jnp.pad(b1, (0, padh))[None, :]
    w2p = jnp.pad(w2, ((0, padh), (0, padh)))
    b2p = jnp.pad(b2, (0, padh))[None, :]
    w3p = jnp.pad(w3, ((0, padh), (0, 0)))
    nf = HY_FILTERS * D_MODEL
    fix = lambda i: (0, 0)
    return pl.pallas_call(
        functools.partial(_hy_filter_kernel, tl=tl),
        grid=(seq // tl,),
        in_specs=[pl.BlockSpec((tl, LANES), lambda i: (i, 0)),
                  pl.BlockSpec((LANES, LANES), fix), pl.BlockSpec((1, LANES), fix),
                  pl.BlockSpec((LANES, LANES), fix), pl.BlockSpec((1, LANES), fix),
                  pl.BlockSpec((LANES, nf), fix), pl.BlockSpec((1, nf), fix)],
        out_specs=pl.BlockSpec((HY_FILTERS, tl, D_MODEL), lambda i: (0, i, 0)),
        out_shape=jax.ShapeDtypeStruct((HY_FILTERS, seq, D_MODEL), F32),
        compiler_params=_params("parallel"),
        name="hy_filter",
    )(feat, w1p, b1p, w2p, b2p, w3p, decay.reshape(1, nf))


def _dft_tables(seq):
    n = 2 * seq
    n2 = FFT_N2
    n1 = n // n2
    n1h = n1 // 2
    two_pi = 2.0 * math.pi

    def ang(m, period):
        return (m % period).astype(F32) * (two_pi / period)

    i_n2 = jnp.arange(n2, dtype=jnp.int32)[:, None, None]
    i_k1 = jnp.arange(n1, dtype=jnp.int32)[None, :, None]
    i_n1 = jnp.arange(n1h, dtype=jnp.int32)[None, None, :]
    th = ang(i_k1 * (n2 * i_n1 + i_n2), n)
    fa = jnp.concatenate([jnp.cos(th), -jnp.sin(th)], axis=1)
    a = jnp.arange(n2, dtype=jnp.int32)
    ph = ang(a[:, None] * a[None, :], n2)
    c, s = jnp.cos(ph), jnp.sin(ph)
    mf = jnp.concatenate([jnp.concatenate([c, s], axis=1), jnp.concatenate([-s, c], axis=1)], axis=0)
    j_k1 = jnp.arange(n1, dtype=jnp.int32)[:, None, None]
    j_n2 = jnp.arange(n2, dtype=jnp.int32)[None, :, None]
    j_k2 = jnp.arange(n2, dtype=jnp.int32)[None, None, :]
    phi = ang(j_n2 * (j_k1 + n1 * j_k2), n)
    c, s = jnp.cos(phi), jnp.sin(phi)
    gi = jnp.concatenate([jnp.concatenate([c, -s], axis=2), jnp.concatenate([s, c], axis=2)], axis=1)
    r_n1 = jnp.arange(n1h, dtype=jnp.int32)[:, None]
    r_k1 = jnp.arange(n1, dtype=jnp.int32)[None, :]
    psi = ang(r_n1 * r_k1, n1)
    fb = jnp.concatenate([jnp.cos(psi), -jnp.sin(psi)], axis=1) * (1.0 / n)
    return dict(n1=n1, n2=n2, n1h=n1h, fa=fa, mf=mf, gi=gi, fb=fb)


def _dot(a, b, exact):
    if exact:
        return jnp.dot(a, b, preferred_element_type=F32, precision=lax.Precision.HIGHEST)
    return jnp.dot(a, b.astype(BF16), preferred_element_type=F32)


def _fft_a_kernel(x_ref, fa_ref, o_ref, *, n1, exact):
    for j in range(SUBLANES):
        r = _dot(fa_ref[j], x_ref[:, j, :], exact)
        o_ref[0, :, j, :] = r[:n1]
        o_ref[1, :, j, :] = r[n1:]


def fft_stage_a(x, tab, exact=False):
    bz, seq, d = x.shape
    n1, n2, n1h = tab["n1"], tab["n2"], tab["n1h"]
    g8 = n2 // SUBLANES
    fa = tab["fa"] if exact else tab["fa"].astype(BF16)
    x5 = x.reshape(bz, n1h, g8, SUBLANES, d)
    out = pl.pallas_call(
        functools.partial(_fft_a_kernel, n1=n1, exact=exact),
        grid=(bz, g8),
        in_specs=[pl.BlockSpec((None, n1h, None, SUBLANES, d), lambda b, g: (b, 0, g, 0, 0)),
                  pl.BlockSpec((SUBLANES, 2 * n1, n1h), lambda b, g: (g, 0, 0))],
        out_specs=pl.BlockSpec((None, 2, n1, None, SUBLANES, d), lambda b, g: (b, 0, 0, g, 0, 0)),
        out_shape=jax.ShapeDtypeStruct((bz, 2, n1, g8, SUBLANES, d), F32),
        compiler_params=_params("parallel", "parallel"),
        name="fft_stage_a",
    )(x5, fa)
    return out.reshape(bz, 2, n1, n2, d)


def _filter_spec_kernel(cf_ref, cb_ref, mf_ref, o_ref, *, n2):
    mf = mf_ref[...]
    yf = _dot(mf, jnp.concatenate([cf_ref[0], cf_ref[1]], axis=0), False)
    yb = _dot(mf, jnp.concatenate([cb_ref[0], cb_ref[1]], axis=0), False)
    o_ref[0] = yf[:n2] + yb[:n2]
    o_ref[1] = yf[n2:] - yb[n2:]


def filter_spectrum(cf, tab):
    n1, n2 = tab["n1"], tab["n2"]
    d = cf.shape[-1]
    norder = HY_FILTERS // 2
    blk = (None, 2, None, n2, d)
    return pl.pallas_call(
        functools.partial(_filter_spec_kernel, n2=n2),
        grid=(norder, n1),
        in_specs=[pl.BlockSpec(blk, lambda o, k: (o, 0, k, 0, 0)),
                  pl.BlockSpec(blk, lambda o, k: (norder + o, 0, k, 0, 0)),
                  pl.BlockSpec((2 * n2, 2 * n2), lambda o, k: (0, 0))],
        out_specs=pl.BlockSpec(blk, lambda o, k: (o, 0, k, 0, 0)),
        out_shape=jax.ShapeDtypeStruct((norder, 2, n1, n2, d), F32),
        compiler_params=_params("parallel", "parallel"),
        name="filter_spectrum",
    )(cf, cf, tab["mf"].astype(BF16))


def _fft_mid_kernel(c_ref, kf_ref, mf_ref, gi_ref, o_ref, *, n2):
    c = jnp.concatenate([c_ref[0], c_ref[1]], axis=0)
    y = _dot(mf_ref[...], c, False)
    yr, yi = y[:n2], y[n2:]
    kr, ki = kf_ref[0], kf_ref[1]
    z = jnp.concatenate([yr * kr - yi * ki, yr * ki + yi * kr], axis=0)
    dd = _dot(gi_ref[...], z, False)
    o_ref[0] = dd[:n2]
    o_ref[1] = dd[n2:]


def fft_mid(c, kf, tab):
    bz, _, n1, n2, d = c.shape
    blk = (None, 2, None, n2, d)
    return pl.pallas_call(
        functools.partial(_fft_mid_kernel, n2=n2),
        grid=(n1, bz),
        in_specs=[pl.BlockSpec(blk, lambda k, b: (b, 0, k, 0, 0)),
                  pl.BlockSpec((2, None, n2, d), lambda k, b: (0, k, 0, 0)),
                  pl.BlockSpec((2 * n2, 2 * n2), lambda k, b: (0, 0)),
                  pl.BlockSpec((None, 2 * n2, 2 * n2), lambda k, b: (k, 0, 0))],
        out_specs=pl.BlockSpec(blk, lambda k, b: (b, 0, k, 0, 0)),
        out_shape=jax.ShapeDtypeStruct((bz, 2, n1, n2, d), F32),
        compiler_params=_params("parallel", "parallel"),
        name="fft_mid",
    )(c, kf, tab["mf"].astype(BF16), tab["gi"].astype(BF16))


def _fft_b_kernel(d_ref, fb_ref, z_ref, g_ref, skip_ref, o_ref):
    fb = fb_ref[...]
    skip = skip_ref[...]
    for j in range(SUBLANES):
        dd = jnp.concatenate([d_ref[0, :, j, :], d_ref[1, :, j, :]], axis=0)
        y = _dot(fb, dd, False)
        o_ref[:, j, :] = (y + skip * z_ref[:, j, :]) * g_ref[:, j, :]


def fft_stage_b(dp, tab, z, gate, skip):
    bz, _, n1, n2, d = dp.shape
    n1h = tab["n1h"]
    g8 = n2 // SUBLANES
    seq = z.shape[1]
    dp6 = dp.reshape(bz, 2, n1, g8, SUBLANES, d)
    view = lambda a: a.reshape(bz, n1h, g8, SUBLANES, d)
    tok = pl.BlockSpec((None, n1h, None, SUBLANES, d), lambda b, g: (b, 0, g, 0, 0))
    out = pl.pallas_call(
        _fft_b_kernel,
        grid=(bz, g8),
        in_specs=[pl.BlockSpec((None, 2, n1, None, SUBLANES, d), lambda b, g: (b, 0, 0, g, 0, 0)),
                  pl.BlockSpec((n1h, 2 * n1), lambda b, g: (0, 0)),
                  tok, tok,
                  pl.BlockSpec((1, d), lambda b, g: (0, 0))],
        out_specs=tok,
        out_shape=jax.ShapeDtypeStruct((bz, n1h, g8, SUBLANES, d), F32),
        compiler_params=_params("parallel", "parallel"),
        name="fft_stage_b",
    )(dp6, tab["fb"].astype(BF16), view(z), view(gate), skip)
    return out.reshape(bz, seq, d)


def hyena_mixer(xf, bsz, seq, w, j):
    d = D_MODEL
    uvg = hy_inproj(xf, w["hy_w_in"][j].astype(BF16), w["hy_b_in"][j][None, :],
                    w["hy_w_conv"][j], w["hy_b_conv"][j][None, :], seq)
    uvg = uvg.reshape(3, bsz, seq, d)
    tab = _dft_tables(seq)
    hk = hy_filter(seq, w["hy_f_w1"][j], w["hy_f_b1"][j], w["hy_f_w2"][j], w["hy_f_b2"][j],
                   w["hy_f_w3"][j], w["hy_decay"][j])
    kf = filter_spectrum(fft_stage_a(hk, tab), tab)
    z = uvg[0]
    for order in range(2):
        c = fft_stage_a(z, tab)
        dp = fft_mid(c, kf[order], tab)
        z = fft_stage_b(dp, tab, z, uvg[1 + order], w["hy_skip"][j][order][None, :])
    return z.reshape(bsz * seq, d)


def _na_bias_kernel(rpb_ref, o_ref):
    h = pl.program_id(0)
    q = lax.broadcasted_iota(jnp.int32, (GRID_W, GRID_W), 0)
    kc = lax.broadcasted_iota(jnp.int32, (GRID_W, GRID_W), 1)
    win_start = jnp.clip(q - WIN_C // 2, 0, GRID_W - WIN_C)
    valid = (kc >= win_start) & (kc < win_start + WIN_C)
    dc = jnp.clip(kc - q + WIN_C - 1, 0, 2 * WIN_C - 2)
    nrow = 2 * WIN_R - 1
    for dr in range(nrow):
        t = jnp.zeros((GRID_W, GRID_W), F32)
        for c in range(2 * WIN_C - 1):
            t = jnp.where(dc == c, rpb_ref[h * nrow + dr, c], t)
        o_ref[dr] = jnp.where(valid, t, NEG_INF)


def na_bias_table(rpb):
    nrow = 2 * WIN_R - 1
    ncol = 2 * WIN_C - 1
    return pl.pallas_call(
        _na_bias_kernel,
        grid=(NA_HEADS,),
        in_specs=[pl.BlockSpec(memory_space=pltpu.SMEM)],
        out_specs=pl.BlockSpec((None, nrow, GRID_W, GRID_W), lambda h: (h, 0, 0, 0)),
        out_shape=jax.ShapeDtypeStruct((NA_HEADS, nrow, GRID_W, GRID_W), F32),
        compiler_params=_params("parallel"),
        name="na_bias",
    )(rpb.reshape(NA_HEADS * nrow, ncol))


def _na_attn_kernel(q_ref, k_ref, v_ref, *rest, rows):
    bias_refs, o_ref = rest[:-1], rest[-1]
    step = pl.program_id(2)
    gl = NA_HEAD_GROUP * NA_HEAD_DIM
    nk = WIN_R * GRID_W
    stacked = NA_HEAD_GROUP * GRID_W
    row_head = lax.broadcasted_iota(jnp.int32, (stacked, gl), 0) // GRID_W
    lane_head = lax.broadcasted_iota(jnp.int32, (stacked, gl), 1) // NA_HEAD_DIM
    own = row_head == lane_head
    out_head = lax.broadcasted_iota(jnp.int32, (GRID_W, gl), 1) // NA_HEAD_DIM
    for t, bias_ref in enumerate(bias_refs):
        r = step * NA_ROWS + t
        rs = jnp.clip(r - WIN_R // 2, 0, rows - WIN_R)
        start = pl.multiple_of(rs * GRID_W, GRID_W)
        kb = k_ref[pl.ds(start, nk), :]
        vb = v_ref[pl.ds(start, nk), :]
        q = q_ref[t * GRID_W:(t + 1) * GRID_W, :]
        qs = jnp.where(own, jnp.concatenate([q] * NA_HEAD_GROUP, axis=0), jnp.zeros((), q.dtype))
        s = lax.dot_general(qs, kb, (((1,), (1,)), ((), ())), preferred_element_type=F32)
        s = s + bias_ref[...].reshape(stacked, nk)
        m = jnp.max(s, axis=-1, keepdims=True)
        e = jnp.exp(s - m)
        p = e * (1.0 / jnp.sum(e, axis=-1, keepdims=True))
        oall = jnp.dot(p.astype(BF16), vb, preferred_element_type=F32)
        o = jnp.zeros((GRID_W, gl), F32)
        for hh in range(NA_HEAD_GROUP):
            o = jnp.where(out_head == hh, oall[hh * GRID_W:(hh + 1) * GRID_W, :], o)
        o_ref[t * GRID_W:(t + 1) * GRID_W, :] = o


def na_attention(qkv, bias, bsz, seq):
    rows = seq // GRID_W
    t = bsz * seq
    gl = NA_HEAD_GROUP * NA_HEAD_DIM
    ngrp = NA_HEADS // NA_HEAD_GROUP

    def offset(r):
        return r - jnp.clip(r - WIN_R // 2, 0, rows - WIN_R)

    nstep = rows // NA_ROWS
    tok = pl.BlockSpec((NA_ROWS * GRID_W, gl), lambda b, g, r: (b * nstep + r, g))

    def bias_spec(t):
        return pl.BlockSpec((NA_HEAD_GROUP, None, GRID_W, WIN_R * GRID_W),
                            lambda b, g, r: (g, offset(r * NA_ROWS + t), 0, 0))

    return pl.pallas_call(
        functools.partial(_na_attn_kernel, rows=rows),
        grid=(bsz, ngrp, nstep),
        in_specs=[tok,
                  pl.BlockSpec((seq, gl), lambda b, g, r: (b, ngrp + g)),
                  pl.BlockSpec((seq, gl), lambda b, g, r: (b, 2 * ngrp + g))]
                 + [bias_spec(t) for t in range(NA_ROWS)],
        out_specs=tok,
        out_shape=jax.ShapeDtypeStruct((t, D_MODEL), F32),
        compiler_params=_params("parallel", "parallel", "arbitrary"),
        name="na_attention",
    )(qkv, qkv, qkv, *([bias] * NA_ROWS))


def na_mixer(xf, bsz, seq, w, j):
    d = D_MODEL
    colscale = jnp.concatenate([jnp.full((1, d), NA_HEAD_DIM ** -0.5, F32), jnp.ones((1, 2 * d), F32)], axis=1)
    qkv = mm_bias(xf, w["na_w_qkv"][j].astype(BF16), w["na_b_qkv"][j][None, :], colscale, BF16)
    tb = na_bias_table(w["na_rpb"][j])
    dr = jnp.arange(WIN_R)[None, :] - jnp.arange(WIN_R)[:, None] + WIN_R - 1
    bias = tb[:, dr]
    bias = bias.transpose(0, 1, 3, 2, 4).reshape(NA_HEADS, WIN_R, GRID_W, WIN_R * GRID_W)
    return na_attention(qkv, bias, bsz, seq)


def _pair_candidates(v1, i1, v2, i2):
    tt = v1.shape[1]
    sub = lax.broadcasted_iota(jnp.int32, (SUBLANES, tt), 0)
    vals, keys, ids = [], [], []
    for a in range(PEER_TOPK // 2):
        nb = PEER_TOPK // (a + 1)
        va = v1[a:a + 1, :]
        ia = i1[a:a + 1, :] * PEER_KEYS
        for b0 in range(0, nb, SUBLANES):
            val = va + v2[b0:b0 + SUBLANES, :]
            if nb - b0 < SUBLANES:
                val = jnp.where(sub < nb - b0, val, -jnp.inf)
            vals.append(val)
            keys.append(a * PEER_TOPK + b0 + sub)
            ids.append(ia + i2[b0:b0 + SUBLANES, :])
    half = PEER_TOPK // 2
    vals.append(v1[half:, :] + v2[0:1, :])
    keys.append((half + sub) * PEER_TOPK)
    ids.append(i1[half:, :] * PEER_KEYS + i2[0:1, :])
    cat = lambda xs: jnp.concatenate(xs, axis=0)
    return cat(vals), cat(keys), cat(ids)


PEER_GROUP = SUBLANES
PEER_SETS = 4
PEER_AHEAD = 2
PEER_TILE = 128
PEER_RING = 3
PEER_SPLIT = PEER_TOPK // 2


def _round(s, key):
    m = jnp.max(s, axis=0, keepdims=True)
    am = jnp.min(jnp.where(s == m, key, jnp.iinfo(jnp.int32).max), axis=0, keepdims=True)
    hit = key == am
    return m, am, hit, jnp.where(hit, -jnp.inf, s)


def _peer_fused_kernel(x_ref, xn1_ref, xn2_ref, wq_ref, k1_ref, k2_ref, tab_ref, o_ref,
                       buf, sem, idx_s, gate_s, stage_i, stage_g, s2_s, v1_s, i1_s, v2_s, i2_s, tr_i, dsem,
                       *, ntile):
    d = D_MODEL
    tt = PEER_TILE
    ngroup = tt // PEER_GROUP
    half_rows = PEER_PAIRS // 2
    tile = pl.program_id(0)
    cat = lambda xs: jnp.concatenate(xs, axis=0)
    nt_dims = (((1,), (1,)), ((), ()))

    def ring(k):
        return lax.rem(tile + k, PEER_RING)

    def route_a(h, xr_ref):
        q = jnp.dot(xr_ref[...].astype(BF16), wq_ref[h], preferred_element_type=F32).astype(BF16)
        s1 = lax.dot_general(k1_ref[...], q[:, :PEER_HALF], nt_dims, preferred_element_type=F32)
        s2 = lax.dot_general(k2_ref[...], q[:, PEER_HALF:], nt_dims, preferred_element_type=F32)
        rowid = lax.broadcasted_iota(jnp.int32, s1.shape, 0)
        yield
        vals, ids = [], []
        for _ in range(PEER_TOPK):
            m, am, _, s1 = _round(s1, rowid)
            vals.append(m)
            ids.append(am)
            yield
        vals2, ids2 = [], []
        for _ in range(PEER_SPLIT):
            m, am, _, s2 = _round(s2, rowid)
            vals2.append(m)
            ids2.append(am)
            yield
        v1_s[...] = cat(vals)
        i1_s[...] = cat(ids)
        v2_s[0:PEER_SPLIT, :] = cat(vals2)
        i2_s[0:PEER_SPLIT, :] = cat(ids2)
        s2_s[...] = s2

    def route_b(h):
        s2 = s2_s[...]
        v1 = v1_s[...]
        i1 = i1_s[...]
        v2lo = v2_s[0:PEER_SPLIT, :]
        i2lo = i2_s[0:PEER_SPLIT, :]
        rowid = lax.broadcasted_iota(jnp.int32, s2.shape, 0)
        yield
        vals2, ids2 = [], []
        for _ in range(PEER_TOPK - PEER_SPLIT):
            m, am, _, s2 = _round(s2, rowid)
            vals2.append(m)
            ids2.append(am)
            yield
        v2 = cat([v2lo] + vals2)
        i2 = cat([i2lo] + ids2)
        cand, ckey, cidx = _pair_candidates(v1, i1, v2, i2)
        yield
        tops, tids = [], []
        for _ in range(PEER_TOPK):
            m, _, hit, cand = _round(cand, ckey)
            tops.append(m)
            tids.append(jnp.max(jnp.where(hit, cidx, -1), axis=0, keepdims=True))
            yield
        top = cat(tops)
        e = jnp.exp(top - top[0:1, :])
        row0 = h * PEER_TOPK
        rows = pl.ds(row0 if isinstance(row0, int) else pl.multiple_of(row0, PEER_TOPK), PEER_TOPK)
        stage_g[rows, :] = e / jnp.sum(e, axis=0, keepdims=True)
        stage_i[rows, :] = cat(tids)

    def publish(slot):
        gate_s[slot] = stage_g[...].T
        tr_i[...] = stage_i[...].T
        cp = pltpu.make_async_copy(tr_i, idx_s.at[slot], dsem.at[0])
        cp.start()
        cp.wait()

    def issue(islot, tok, slot, r0, r1):
        for r in range(r0, r1):
            e = idx_s[islot, tok, r]
            cp = pltpu.make_async_copy(tab_ref.at[e], buf.at[slot, pl.ds(r, 1), :], sem.at[slot])
            cp.start(priority=r % 2)

    def wait(slot):
        pltpu.make_async_copy(tab_ref.at[pl.ds(0, PEER_PAIRS), 0, :], buf.at[slot], sem.at[slot]).wait()

    def step(g, gset, islot, ptok0, routing):
        base = gset * PEER_GROUP
        nxt = ((gset + PEER_AHEAD) % PEER_SETS) * PEER_GROUP
        cur = ring(0)
        for j in range(PEER_GROUP):
            wait(base + j)
        next(routing, None)
        tok0 = pl.multiple_of(g * PEER_GROUP, PEER_GROUP)
        x8 = x_ref[pl.ds(tok0, PEER_GROUP), :].astype(BF16)
        rowid = lax.broadcasted_iota(jnp.int32, (PEER_GROUP, PEER_PAIRS), 0)
        act = jnp.zeros((PEER_GROUP, PEER_PAIRS), F32)
        for j in range(PEER_GROUP):
            issue(islot, ptok0 + j, nxt + j, 0, half_rows)
            u = buf[base + j, :, 0:d].astype(BF16)
            a = lax.dot_general(x8, u, nt_dims, preferred_element_type=F32)
            act = jnp.where(rowid == j, a, act)
            next(routing, None)
            next(routing, None)
        gelu = 0.5 * act * (1.0 + lax.erf(act * (2.0 ** -0.5)))
        cb = (gate_s[cur, pl.ds(tok0, PEER_GROUP), :] * gelu).astype(BF16)
        orow = lax.broadcasted_iota(jnp.int32, (PEER_GROUP, d), 0)
        out = jnp.zeros((PEER_GROUP, d), F32)
        for j in range(PEER_GROUP):
            issue(islot, ptok0 + j, nxt + j, half_rows, PEER_PAIRS)
            v = buf[base + j, :, d:2 * d].astype(BF16)
            o = jnp.dot(cb, v, preferred_element_type=F32)
            out = jnp.where(orow == j, o, out)
            next(routing, None)
            next(routing, None)
        for _ in routing:
            pass
        o_ref[pl.ds(tok0, PEER_GROUP), :] = out

    def routing_for(g, k):
        h = g // 2 if isinstance(g, int) else lax.shift_right_logical(g, 1)
        return route_a(h, xn2_ref) if k % 2 == 0 else route_b(h)

    @pl.when(tile == 0)
    def _():
        for slot, xr_ref in ((0, x_ref), (1, xn1_ref)):
            def head(h, carry, xr_ref=xr_ref):
                for _ in route_a(h, xr_ref):
                    pass
                for _ in route_b(h):
                    pass
                return carry

            lax.fori_loop(0, PEER_HEADS, head, 0)
            publish(slot)
        for g in range(PEER_AHEAD):
            for j in range(PEER_GROUP):
                issue(0, g * PEER_GROUP + j, g * PEER_GROUP + j, 0, PEER_PAIRS)

    def body(gg, carry):
        for k in range(PEER_SETS):
            g = gg * PEER_SETS + k
            step(g, k, ring(0), (g + PEER_AHEAD) * PEER_GROUP, routing_for(g, k))
        return carry

    lax.fori_loop(0, ngroup // PEER_SETS - 1, body, 0)
    for k in range(PEER_SETS):
        g = ngroup - PEER_SETS + k
        ahead = g + PEER_AHEAD
        if ahead < ngroup:
            step(g, k, ring(0), ahead * PEER_GROUP, routing_for(g, k))
        else:
            step(g, k, ring(1), (ahead - ngroup) * PEER_GROUP, routing_for(g, k))
    publish(ring(2))

    @pl.when(tile == ntile - 1)
    def _():
        for k in range(PEER_AHEAD):
            for j in range(PEER_GROUP):
                wait(k * PEER_GROUP + j)


def peer_fused(x, wq, k1, k2, tab):
    t, d = x.shape
    tt = PEER_TILE
    ntile = t // tt
    ngroup = tt // PEER_GROUP
    assert t % tt == 0 and ntile >= PEER_RING
    assert ngroup % PEER_SETS == 0 and ngroup >= 2 * PEER_SETS and ngroup == 2 * PEER_HEADS
    ahead = lambda k: (lambda i: (jnp.minimum(i + k, ntile - 1), 0))
    fix2 = lambda i: (0, 0)
    f32s = pltpu.VMEM((PEER_TOPK, tt), F32)
    i32s = pltpu.VMEM((PEER_TOPK, tt), jnp.int32)
    return pl.pallas_call(
        functools.partial(_peer_fused_kernel, ntile=ntile),
        grid=(ntile,),
        in_specs=[pl.BlockSpec((tt, d), ahead(0)), pl.BlockSpec((tt, d), ahead(1)), pl.BlockSpec((tt, d), ahead(2)),
                  pl.BlockSpec((PEER_HEADS, d, PEER_QDIM), lambda i: (0, 0, 0)),
                  pl.BlockSpec((PEER_KEYS, PEER_HALF), fix2), pl.BlockSpec((PEER_KEYS, PEER_HALF), fix2),
                  pl.BlockSpec(memory_space=pl.ANY)],
        out_specs=pl.BlockSpec((tt, d), ahead(0)),
        out_shape=jax.ShapeDtypeStruct((t, d), F32),
        scratch_shapes=[pltpu.VMEM((PEER_SETS * PEER_GROUP, PEER_PAIRS, 2 * d), F32),
                        pltpu.SemaphoreType.DMA((PEER_SETS * PEER_GROUP,)),
                        pltpu.SMEM((PEER_RING, tt, PEER_PAIRS), jnp.int32),
                        pltpu.VMEM((PEER_RING, tt, PEER_PAIRS), F32),
                        pltpu.VMEM((PEER_PAIRS, tt), jnp.int32),
                        pltpu.VMEM((PEER_PAIRS, tt), F32),
                        pltpu.VMEM((PEER_KEYS, tt), F32),
                        f32s, i32s, f32s, i32s,
                        pltpu.VMEM((tt, PEER_PAIRS), jnp.int32),
                        pltpu.SemaphoreType.DMA((1,))],
        compiler_params=_params("arbitrary"),
        name="peer_fused",
    )(x, x, x, wq, k1, k2, tab)


def peer_ffn(xf, w, i):
    wq = w["peer_w_q"][i].astype(BF16).reshape(D_MODEL, PEER_HEADS, PEER_QDIM).transpose(1, 0, 2)
    tab = jnp.concatenate([w["peer_u"][i], w["peer_v"][i]], axis=1)[:, None, :]
    return peer_fused(xf, wq, w["peer_k1"][i].astype(BF16), w["peer_k2"][i].astype(BF16), tab)


def trunk(x, p, w):
    bsz, seq, d = x.shape
    t = bsz * seq
    xf = x.reshape(t, d)
    pf = p.reshape(DEPTH, t, PLE_DIM)
    for i in range(DEPTH):
        j = i // 2
        if i % 2 == 0:
            mix = hyena_mixer(xf, bsz, seq, w, j)
            w_out, b_out = w["hy_w_out"][j], w["hy_b_out"][j]
        else:
            mix = na_mixer(xf, bsz, seq, w, j)
            w_out, b_out = w["na_w_out"][j], w["na_b_out"][j]
        xf = outproj_ln(mix, w_out.astype(BF16), b_out[None, :], xf, w["ln1_g"][i][None, :], w["ln1_b"][i][None, :])
        ffn = peer_ffn(xf, w, i)
        xf = ln_ple(xf, ffn, pf[i], w["ple_gate_w"][i].astype(BF16), w["ple_w"][i].astype(BF16),
                    w["ln2_g"][i][None, :], w["ln2_b"][i][None, :])
    return xf.reshape(bsz, seq, d)


def kernel(x_prompt, x_sample, p_prompt, p_sample, hy_w_in, hy_b_in, hy_w_conv, hy_b_conv, hy_f_w1, hy_f_b1,
           hy_f_w2, hy_f_b2, hy_f_w3, hy_decay, hy_skip, hy_w_out, hy_b_out, na_w_qkv, na_b_qkv, na_rpb,
           na_w_out, na_b_out, ln1_g, ln1_b, ln2_g, ln2_b, peer_w_q, peer_k1, peer_k2, peer_u, peer_v,
           ple_w, ple_gate_w):
    w = dict(hy_w_in=hy_w_in, hy_b_in=hy_b_in, hy_w_conv=hy_w_conv, hy_b_conv=hy_b_conv,
             hy_f_w1=hy_f_w1, hy_f_b1=hy_f_b1, hy_f_w2=hy_f_w2, hy_f_b2=hy_f_b2, hy_f_w3=hy_f_w3,
             hy_decay=hy_decay, hy_skip=hy_skip, hy_w_out=hy_w_out, hy_b_out=hy_b_out,
             na_w_qkv=na_w_qkv, na_b_qkv=na_b_qkv, na_rpb=na_rpb, na_w_out=na_w_out, na_b_out=na_b_out,
             ln1_g=ln1_g, ln1_b=ln1_b, ln2_g=ln2_g, ln2_b=ln2_b,
             peer_w_q=peer_w_q, peer_k1=peer_k1, peer_k2=peer_k2, peer_u=peer_u, peer_v=peer_v,
             ple_w=ple_w, ple_gate_w=ple_gate_w)
    return (trunk(x_prompt, p_prompt, w), trunk(x_sample, p_sample, w))
```
